```python
import jax
import jax.numpy as jnp
from jax import lax
import numpy as np

D_MODEL = 4096
BATCH = 16
SEQ = 256
DEPTH = 2
DEC_BATCH = 8
DEC_SEQ = 4096
PAST_LEN = 256

GRID_W = 64
D_GLA = D_MODEL // 2
D_GDN = D_MODEL - D_GLA
GLA_HEADS = 4
GLA_DV = D_GLA // GLA_HEADS
GLA_DK = GLA_DV // 2
GLA_QK = GLA_HEADS * GLA_DK
GLA_RANK = 16
GLA_TAU = 16.0
GDN_HEADS = 16
GDN_DK = D_GDN // GDN_HEADS
GDN_DV = GDN_DK
CONV_K = 3
CHUNK = 64
N_GROUPS = 4
EXP_PER_GROUP = 4
N_EXPERTS = N_GROUPS * EXP_PER_GROUP
TOP_K = 2
D_FF_EXPERT = D_MODEL // 2
MOE_BLOCK = 512
N_MOD = 6
EPS = 1e-6
IN_SPLITS = (GLA_QK, GLA_QK, D_GLA, D_GLA, 2 * GLA_RANK, 3 * D_GDN, D_GDN, 2 * GDN_HEADS, 2 * GDN_HEADS)
N_IN = sum(IN_SPLITS)

kernel_name = 'hybrid_gla_gdn_moe_diffusion_step'


def rmsnorm(x, w):
    xf = x.astype(jnp.float32)
    y = xf * lax.rsqrt(jnp.mean(xf * xf, axis=-1, keepdims=True) + EPS)
    return (y * w.astype(jnp.float32)).astype(x.dtype)


def head_rmsnorm(o, w):
    return o * lax.rsqrt(jnp.mean(o * o, axis=-1, keepdims=True) + EPS) * w.astype(jnp.float32)


def l2norm(x):
    return x * lax.rsqrt(jnp.sum(x * x, axis=-1, keepdims=True) + EPS)


def to_heads(x, n_heads):
    b_, l_, _ = x.shape
    return x.reshape(b_, l_, n_heads, -1).transpose(0, 2, 1, 3)


def from_heads(o):
    b_, h_, l_, d_ = o.shape
    return o.transpose(0, 2, 1, 3).reshape(b_, l_, h_ * d_)


def flip(t):
    return jnp.flip(t, axis=2)


def grid_dwconv(x, w, rows):
    b_, l_, ch = x.shape
    xg = x.reshape(b_, rows, l_ // rows, ch)
    y = lax.conv_general_dilated(xg, w.astype(x.dtype)[:, :, None, :], window_strides=(1, 1), padding='SAME',
                                 dimension_numbers=('NHWC', 'HWIO', 'NHWC'), feature_group_count=ch)
    return y.reshape(b_, l_, ch)


def gla_chunk(q, k, v, g, s0):
    b_, h_, l_, dk = q.shape
    dv = v.shape[-1]
    n = l_ // CHUNK
    q, k, g = (t.reshape(b_, h_, n, CHUNK, dk) for t in (q, k, g))
    v = v.reshape(b_, h_, n, CHUNK, dv)
    cum = jnp.cumsum(g, axis=3)
    ref = cum[:, :, :, CHUNK // 2:CHUNK // 2 + 1, :]
    last = cum[:, :, :, CHUNK - 1:, :]
    causal = jnp.tril(jnp.ones((CHUNK, CHUNK), bool))
    att = jnp.einsum('bhnid,bhnjd->bhnij', q * jnp.exp(cum - ref), k * jnp.exp(ref - cum))
    o_intra = jnp.einsum('bhnij,bhnjv->bhniv', jnp.where(causal, att, 0.0), v)
    q_dec = q * jnp.exp(cum)
    k_dec = k * jnp.exp(last - cum)
    chunk_dec = jnp.exp(last[:, :, :, 0, :])

    def step(s, inp):
        q_n, k_n, v_n, d_n = inp
        o_n = jnp.einsum('bhcd,bhdv->bhcv', q_n, s)
        s = d_n[..., None] * s + jnp.einsum('bhcd,bhcv->bhdv', k_n, v_n)
        return s, o_n

    xs = tuple(jnp.moveaxis(t, 2, 0) for t in (q_dec, k_dec, v, chunk_dec))
    s_fin, o_inter = lax.scan(step, s0, xs)
    o = o_intra + jnp.moveaxis(o_inter, 0, 2)
    return o.reshape(b_, h_, l_, dv), s_fin


def gdn_chunk(q, k, v, g, beta, s0):
    b_, h_, l_, dk = q.shape
    dv = v.shape[-1]
    n = l_ // CHUNK
    q, k = (t.reshape(b_, h_, n, CHUNK, dk) for t in (q, k))
    v = v.reshape(b_, h_, n, CHUNK, dv)
    g, beta = (t.reshape(b_, h_, n, CHUNK) for t in (g, beta))
    cum = jnp.cumsum(g, axis=-1)
    incl = jnp.tril(jnp.ones((CHUNK, CHUNK), bool))
    strict = jnp.tril(jnp.ones((CHUNK, CHUNK), bool), -1)
    gamma = jnp.exp(jnp.where(incl, cum[..., :, None] - cum[..., None, :], -jnp.inf))
    kk = jnp.einsum('bhnid,bhnjd->bhnij', k, k)
    m = jnp.where(strict, beta[..., :, None] * kk * gamma, 0.0)
    rhs = jnp.concatenate([v * beta[..., None], k * (beta * jnp.exp(cum))[..., None]], axis=-1)
    sol = lax.linalg.triangular_solve(m + jnp.eye(CHUNK, dtype=m.dtype), rhs, left_side=True, lower=True,
                                      unit_diagonal=True)
    u, w = sol[..., :dv], sol[..., dv:]
    qk = jnp.einsum('bhnid,bhnjd->bhnij', q, k) * gamma
    q_dec = q * jnp.exp(cum)[..., None]
    k_dec = k * jnp.exp(cum[..., -1:] - cum)[..., None]
    chunk_dec = jnp.exp(cum[..., -1])

    def step(s, inp):
        u_n, w_n, qk_n, q_n, k_n, d_n = inp
        v_new = u_n - jnp.einsum('bhcd,bhdv->bhcv', w_n, s)
        o_n = jnp.einsum('bhcd,bhdv->bhcv', q_n, s) + jnp.einsum('bhij,bhjv->bhiv', qk_n, v_new)
        s = d_n[..., None, None] * s + jnp.einsum('bhcd,bhcv->bhdv', k_n, v_new)
        return s, o_n

    xs = tuple(jnp.moveaxis(t, 2, 0) for t in (u, w, qk, q_dec, k_dec, chunk_dec))
    s_fin, o = lax.scan(step, s0, xs)
    return jnp.moveaxis(o, 0, 2).reshape(b_, h_, l_, dv), s_fin


def mixer(h, rows, s_gla, s_gdn, w_in, gla_w2, gla_b2, gla_norm_w, gdn_conv_w, gdn_a_log, gdn_dt_bias,
          gdn_norm_w, w_out):
    f32 = jnp.float32
    b_, l_, _ = h.shape
    proj = h @ w_in
    cuts = np.cumsum(IN_SPLITS)[:-1].tolist()
    gq, gk, gv, gr, glr, dqkv, dz, da, db = jnp.split(proj, cuts, axis=-1)

    q = to_heads(gq, GLA_HEADS).astype(f32) * GLA_DK ** -0.5
    k = to_heads(gk, GLA_HEADS).astype(f32)
    v = to_heads(gv, GLA_HEADS).astype(f32)
    lr = glr.reshape(b_, l_, 2, GLA_RANK)
    gate_pre = jnp.einsum('blsr,srk->sblk', lr, gla_w2) + gla_b2[:, None, None, :]
    glog = jax.nn.log_sigmoid(gate_pre.astype(f32)) / GLA_TAU
    g_f = to_heads(glog[0], GLA_HEADS)
    g_b = to_heads(glog[1], GLA_HEADS)
    o_f, sg_f = gla_chunk(q, k, v, g_f, s_gla[:, 0].astype(f32))
    o_b, sg_b = gla_chunk(flip(q), flip(k), flip(v), flip(g_b), s_gla[:, 1].astype(f32))
    o_gla = head_rmsnorm(o_f + flip(o_b), gla_norm_w)
    gla_out = from_heads(o_gla).astype(h.dtype) * jax.nn.silu(gr)

    qkv = jax.nn.silu(grid_dwconv(dqkv, gdn_conv_w, rows))
    cq, ck, cv = jnp.split(qkv, 3, axis=-1)
    q2 = l2norm(to_heads(cq, GDN_HEADS).astype(f32)) * GDN_DK ** -0.5
    k2 = l2norm(to_heads(ck, GDN_HEADS).astype(f32))
    v2 = to_heads(cv, GDN_HEADS).astype(f32)
    a = da.reshape(b_, l_, 2, GDN_HEADS).astype(f32)
    bb = db.reshape(b_, l_, 2, GDN_HEADS).astype(f32)
    g = -jnp.exp(gdn_a_log.astype(f32)) * jax.nn.softplus(a + gdn_dt_bias.astype(f32))
    g = g.transpose(2, 0, 3, 1)
    beta = jax.nn.sigmoid(bb).transpose(2, 0, 3, 1)
    o2_f, sd_f = gdn_chunk(q2, k2, v2, g[0], beta[0], s_gdn[:, 0].astype(f32))
    o2_b, sd_b = gdn_chunk(flip(q2), flip(k2), flip(v2), flip(g[1]), flip(beta[1]), s_gdn[:, 1].astype(f32))
    o_gdn = head_rmsnorm(o2_f + flip(o2_b), gdn_norm_w)
    gdn_out = from_heads(o_gdn).astype(h.dtype) * jax.nn.silu(dz)

    out = jnp.concatenate([gla_out, gdn_out], axis=-1) @ w_out
    return out, jnp.stack([sg_f, sg_b], axis=1), jnp.stack([sd_f, sd_b], axis=1)


def route(h, router_w, router_bias):
    probs = jax.nn.softmax((h @ router_w).astype(jnp.float32), axis=-1)
    sel = probs + router_bias.astype(jnp.float32)
    group_score = lax.top_k(sel.reshape(-1, N_GROUPS, EXP_PER_GROUP), TOP_K)[0].sum(-1)
    best = jnp.argmax(group_score, axis=-1)
    in_group = (jnp.arange(N_EXPERTS) // EXP_PER_GROUP)[None, :] == best[:, None]
    _, idx = lax.top_k(jnp.where(in_group, sel, -jnp.inf), TOP_K)
    gate = jnp.take_along_axis(probs, idx, axis=-1)
    gate = gate / jnp.sum(gate, axis=-1, keepdims=True)
    return idx.astype(jnp.int32), gate


def moe_ffn(h, router_w, router_bias, w1, w3, w2):
    t_, d_ = h.shape
    idx, gate = route(h, router_w, router_bias)
    n_assign = t_ * TOP_K
    flat_e = idx.reshape(-1)
    order = jnp.argsort(flat_e)
    sorted_e = flat_e[order]
    counts = jnp.zeros((N_EXPERTS,), jnp.int32).at[flat_e].add(1)
    padded = (counts + MOE_BLOCK - 1) // MOE_BLOCK * MOE_BLOCK
    pad_end = jnp.cumsum(padded)
    pad_start = pad_end - padded
    start = jnp.cumsum(counts) - counts
    dest = pad_start[sorted_e] + jnp.arange(n_assign, dtype=jnp.int32) - start[sorted_e]
    n_blocks = -(-n_assign // MOE_BLOCK) + N_EXPERTS
    n_slots = n_blocks * MOE_BLOCK
    slot_tok = jnp.full((n_slots,), t_, jnp.int32).at[dest].set((order // TOP_K).astype(jnp.int32))
    slot_gate = jnp.zeros((n_slots,), jnp.float32).at[dest].set(gate.reshape(-1)[order])
    block_e = jnp.minimum(jnp.searchsorted(pad_end, jnp.arange(n_blocks, dtype=jnp.int32) * MOE_BLOCK,
                                           side='right'), N_EXPERTS - 1)
    h_pad = jnp.concatenate([h, jnp.zeros((1, d_), h.dtype)], axis=0)

    def run_block(args):
        e, tok, gt = args
        xb = h_pad[tok]
        act = jax.nn.silu(xb @ w1[e]) * (xb @ w3[e])
        return (act @ w2[e]) * gt[:, None].astype(xb.dtype)

    out = lax.map(run_block, (block_e, slot_tok.reshape(n_blocks, MOE_BLOCK), slot_gate.reshape(n_blocks, MOE_BLOCK)))
    y = jnp.zeros((t_ + 1, d_), h.dtype).at[slot_tok].add(out.reshape(n_slots, d_))
    return y[:t_]


def trunk_layer(x, mod, rows, s_gla, s_gdn, ln1, w_in, gla_w2, gla_b2, gla_norm_w, gdn_conv_w, gdn_a_log,
                gdn_dt_bias, gdn_norm_w, w_out, ln2, router_w, router_bias, w1, w3, w2):
    shift1, scale1, gate1, shift2, scale2, gate2 = jnp.split(mod[:, None, :].astype(x.dtype), N_MOD, axis=-1)
    h = rmsnorm(x, ln1) * (1 + scale1) + shift1
    mix, s_gla, s_gdn = mixer(h, rows, s_gla, s_gdn, w_in, gla_w2, gla_b2, gla_norm_w, gdn_conv_w, gdn_a_log,
                              gdn_dt_bias, gdn_norm_w, w_out)
    x = x + gate1 * mix
    h = rmsnorm(x, ln2) * (1 + scale2) + shift2
    ffn = moe_ffn(h.reshape(-1, x.shape[-1]), router_w, router_bias, w1, w3, w2).reshape(x.shape)
    x = x + gate2 * ffn
    return x, s_gla, s_gdn


def setup_inputs(seed: int = 0) -> dict:
    key = jax.random.key(seed)
    ks = jax.random.split(key, 26)
    d = D_MODEL

    def nrm(k, shape, s):
        return jax.random.normal(k, shape, jnp.float32) * s

    dt = jnp.exp(jax.random.uniform(ks[15], (DEPTH, 2, GDN_HEADS), jnp.float32, np.log(1e-3), np.log(1e-1)))
    return {
        'x_prompt': nrm(ks[0], (BATCH, SEQ, d), 1.0),
        'x_sample': nrm(ks[1], (DEC_BATCH, DEC_SEQ, d), 1.0),
        'state_gla': nrm(ks[2], (DEC_BATCH, DEPTH, 2, GLA_HEADS, GLA_DK, GLA_DV), 0.5),
        'state_gdn': nrm(ks[3], (DEC_BATCH, DEPTH, 2, GDN_HEADS, GDN_DK, GDN_DV), 0.5),
        'c': nrm(ks[4], (DEC_BATCH, d), 1.0),
        'c_ctx': nrm(ks[5], (d,), 1.0),
        'ln1_w': 1.0 + nrm(ks[6], (DEPTH, d), 0.02),
        'w_mod': nrm(ks[7], (DEPTH, d, N_MOD * d), 0.5 * d ** -0.5),
        'b_mod': nrm(ks[8], (DEPTH, N_MOD * d), 0.02),
        'w_in': nrm(ks[9], (DEPTH, d, N_IN), d ** -0.5),
        'gla_w2': nrm(ks[10], (DEPTH, 2, GLA_RANK, GLA_QK), GLA_RANK ** -0.5),
        'gla_b2': nrm(ks[11], (DEPTH, 2, GLA_QK), 0.1),
        'gla_norm_w': 1.0 + nrm(ks[12], (DEPTH, GLA_DV), 0.02),
        'gdn_conv_w': nrm(ks[13], (DEPTH, CONV_K, CONV_K, 3 * D_GDN), 1.0 / CONV_K),
        'gdn_a_log': jnp.log(jax.random.uniform(ks[14], (DEPTH, 2, GDN_HEADS), jnp.float32, 1.0, 16.0)),
        'gdn_dt_bias': dt + jnp.log(-jnp.expm1(-dt)),
        'gdn_norm_w': 1.0 + nrm(ks[16], (DEPTH, GDN_DV), 0.02),
        'w_out': nrm(ks[17], (DEPTH, D_GLA + D_GDN, d), (D_GLA + D_GDN) ** -0.5),
        'ln2_w': 1.0 + nrm(ks[18], (DEPTH, d), 0.02),
        'router_w': nrm(ks[19], (d, N_EXPERTS), d ** -0.5),
        'router_bias': nrm(ks[20], (N_EXPERTS,), 0.01),
        'w1': nrm(ks[21], (DEPTH, N_EXPERTS, d, D_FF_EXPERT), d ** -0.5),
        'w3': nrm(ks[22], (DEPTH, N_EXPERTS, d, D_FF_EXPERT), d ** -0.5),
        'w2': nrm(ks[23], (DEPTH, N_EXPERTS, D_FF_EXPERT, d), D_FF_EXPERT ** -0.5),
        'final_norm_w': 1.0 + nrm(ks[24], (d,), 0.02),
    }


def reference(x_prompt, x_sample, state_gla, state_gdn, c, c_ctx, ln1_w, w_mod, b_mod, w_in, gla_w2, gla_b2,
              gla_norm_w, gdn_conv_w, gdn_a_log, gdn_dt_bias, gdn_norm_w, w_out, ln2_w, router_w, router_bias,
              w1, w3, w2, final_norm_w):
    f32 = jnp.float32
    ctx_batch = x_prompt.shape[0]
    lat_rows = x_sample.shape[1] // GRID_W
    zero_gla = jnp.zeros((ctx_batch, 2, GLA_HEADS, GLA_DK, GLA_DV), f32)
    zero_gdn = jnp.zeros((ctx_batch, 2, GDN_HEADS, GDN_DK, GDN_DV), f32)
    cond_ctx = jax.nn.silu(c_ctx)[None, :]
    cond_lat = jax.nn.silu(c)
    xp = x_prompt
    xs = x_sample
    gla_states = []
    gdn_states = []
    for l in range(DEPTH):
        lw = (ln1_w[l], w_in[l], gla_w2[l], gla_b2[l], gla_norm_w[l], gdn_conv_w[l], gdn_a_log[l],
              gdn_dt_bias[l], gdn_norm_w[l], w_out[l], ln2_w[l], router_w, router_bias, w1[l], w3[l], w2[l])
        mod_ctx = cond_ctx @ w_mod[l] + b_mod[l]
        mod_lat = cond_lat @ w_mod[l] + b_mod[l]
        xp, s_gla, s_gdn = trunk_layer(xp, mod_ctx, 1, zero_gla, zero_gdn, *lw)
        gla_states.append(s_gla)
        gdn_states.append(s_gdn)
        xs, _, _ = trunk_layer(xs, mod_lat, lat_rows, state_gla[:, l], state_gdn[:, l], *lw)
    y_prompt = rmsnorm(xp, final_norm_w)
    y_sample = rmsnorm(xs, final_norm_w)
    new_state_gla = jnp.stack(gla_states, axis=1)
    new_state_gdn = jnp.stack(gdn_states, axis=1)
    return (y_prompt, y_sample, new_state_gla, new_state_gdn)
```

```python
import functools
import math

import jax
import jax.numpy as jnp
from jax import lax
from jax.experimental import pallas as pl
from jax.experimental.pallas import tpu as pltpu

F32 = jnp.float32
BF16 = jnp.bfloat16
SDS = jax.ShapeDtypeStruct

EPS = 1e-6
CHUNK = 64
GRID_W = 64
GLA_TAU = 16.0
N_GROUPS = 4
TOP_K = 2
N_MOD = 6
LANES = 128
MOE_ROWS = 512
VMEM_LIMIT = 56 * 1024 * 1024

_NT = (((1,), (1,)), ((), ()))
_TN = (((0,), (0,)), ((), ()))


def _params(sem, vmem=VMEM_LIMIT):
    return pltpu.CompilerParams(dimension_semantics=sem, vmem_limit_bytes=vmem)


def _sigmoid(x):
    return 1.0 / (1.0 + jnp.exp(-x))


def _silu(x):
    return x * _sigmoid(x)


def _softplus(x):
    return jnp.maximum(x, 0.0) + jnp.log1p(jnp.exp(-jnp.abs(x)))


def _split_bf16(x):
    hi = x.astype(BF16)
    lo = (x - hi.astype(F32)).astype(BF16)
    return hi, lo


def _tri_dot(tri, x):
    hi, lo = _split_bf16(x)
    return (jnp.dot(tri, hi, preferred_element_type=F32)
            + jnp.dot(tri, lo, preferred_element_type=F32))


def _tri_masks(rev):
    r = lax.broadcasted_iota(jnp.int32, (CHUNK, CHUNK), 0)
    c = lax.broadcasted_iota(jnp.int32, (CHUNK, CHUNK), 1)
    if rev:
        return r <= c, r < c
    return r >= c, r > c


def _mod_kernel(c_ref, w_ref, b_ref, o_ref):
    a = _silu(c_ref[...]).astype(BF16)
    o_ref[...] = jnp.dot(a, w_ref[...].astype(BF16), preferred_element_type=F32) + b_ref[...]


def _mod_call(cond, w_mod, b_mod):
    depth, d, n = w_mod.shape
    r = cond.shape[0]
    tn = 512
    return pl.pallas_call(
        _mod_kernel,
        grid=(depth, n // tn),
        in_specs=[pl.BlockSpec((r, d), lambda l, j: (0, 0)),
                  pl.BlockSpec((None, d, tn), lambda l, j: (l, 0, j)),
                  pl.BlockSpec((None, 1, tn), lambda l, j: (l, 0, j))],
        out_specs=pl.BlockSpec((None, r, tn), lambda l, j: (l, 0, j)),
        out_shape=SDS((depth, r, n), F32),
        compiler_params=_params(("arbitrary", "arbitrary")),
        name="mod",
    )(cond, w_mod, b_mod.reshape(depth, 1, n))


def _in_proj_kernel(x_ref, ln_ref, shift_ref, scale_ref, w_ref, ws_ref, o_ref, os_ref, h_ref):
    @pl.when(pl.program_id(1) == 0)
    def _():
        x = x_ref[...]
        y = x * lax.rsqrt(jnp.mean(x * x, axis=-1, keepdims=True) + EPS) * ln_ref[...]
        h = (y * (1.0 + scale_ref[...]) + shift_ref[...]).astype(BF16)
        h_ref[...] = h
        os_ref[...] = jnp.dot(h, ws_ref[...], preferred_element_type=F32)

    o_ref[...] = jnp.dot(h_ref[...], w_ref[...], preferred_element_type=F32).astype(o_ref.dtype)


def _in_proj_call(x, ln_w, mod3, w_main, w_small, mod_row, tm, tn):
    t, d = x.shape
    nm = w_main.shape[1]
    ns = w_small.shape[1]
    return pl.pallas_call(
        _in_proj_kernel,
        grid=(t // tm, nm // tn),
        in_specs=[pl.BlockSpec((tm, d), lambda i, j: (i, 0)),
                  pl.BlockSpec((1, d), lambda i, j: (0, 0)),
                  pl.BlockSpec((None, 1, d), lambda i, j: (mod_row(i, tm) * N_MOD + 0, 0, 0)),
                  pl.BlockSpec((None, 1, d), lambda i, j: (mod_row(i, tm) * N_MOD + 1, 0, 0)),
                  pl.BlockSpec((d, tn), lambda i, j: (0, j)),
                  pl.BlockSpec((d, ns), lambda i, j: (0, 0))],
        out_specs=[pl.BlockSpec((tm, tn), lambda i, j: (i, j)),
                   pl.BlockSpec((tm, ns), lambda i, j: (i, 0))],
        out_shape=[SDS((t, nm), BF16), SDS((t, ns), F32)],
        scratch_shapes=[pltpu.VMEM((tm, d), BF16)],
        compiler_params=_params(("arbitrary", "arbitrary")),
        name="in_proj",
    )(x, ln_w.reshape(1, d), mod3, mod3, w_main, w_small)


def _gla_dir_step(c, s, first, rank, q_ref, k_ref, v_ref, lr_ref, w2_ref, b2_ref, nw_ref, o_ref, st_ref, oacc_ref):
    rev = s == 1
    dk = q_ref.shape[-1]
    rows = pl.ds(pl.multiple_of(c * CHUNK, CHUNK), CHUNK)
    q = q_ref[rows, :].astype(F32) * dk ** -0.5
    k = k_ref[rows, :].astype(F32)
    v = v_ref[rows, :]
    lr = lr_ref[rows, :][:, s * rank:(s + 1) * rank]
    pre = jnp.dot(lr.astype(BF16), w2_ref[s].astype(BF16), preferred_element_type=F32) + b2_ref[s]
    g = -_softplus(-pre) / GLA_TAU
    incl, _ = _tri_masks(rev)
    cum = _tri_dot(incl.astype(BF16), g)
    ref_i = CHUNK - 1 - CHUNK // 2 if rev else CHUNK // 2
    last_i = 0 if rev else CHUNK - 1
    cref = cum[ref_i:ref_i + 1, :]
    clast = cum[last_i:last_i + 1, :]
    qa = (q * jnp.exp(cum - cref)).astype(BF16)
    ka = (k * jnp.exp(cref - cum)).astype(BF16)
    att = lax.dot_general(qa, ka, _NT, preferred_element_type=F32)
    att = jnp.where(incl, att, 0.0).astype(BF16)
    o = jnp.dot(att, v, preferred_element_type=F32)
    qd = (q * jnp.exp(cum)).astype(BF16)
    kd = (k * jnp.exp(clast - cum)).astype(BF16)
    st = st_ref[s]
    o = o + lax.dot_general(qd, st.astype(BF16), _NT, preferred_element_type=F32)
    st_ref[s] = st * jnp.exp(clast) + lax.dot_general(v, kd, _TN, preferred_element_type=F32)
    if first:
        oacc_ref[rows, :] = o
    else:
        tot = oacc_ref[rows, :] + o
        y = tot * lax.rsqrt(jnp.mean(tot * tot, axis=-1, keepdims=True) + EPS) * nw_ref[...]
        o_ref[rows, :] = y.astype(o_ref.dtype)


def _gla_kernel(*refs, rank, has_init, want_state):
    q_ref, k_ref, v_ref, lr_ref, w2_ref, b2_ref, nw_ref = refs[:7]
    refs = refs[7:]
    if has_init:
        s0_ref, refs = refs[0], refs[1:]
    o_ref, refs = refs[0], refs[1:]
    if want_state:
        sfin_ref, refs = refs[0], refs[1:]
    st_ref, oacc_ref = refs
    n = q_ref.shape[0] // CHUNK
    for s in range(2):
        if has_init:
            st_ref[s] = s0_ref[s].T
        else:
            st_ref[s] = jnp.zeros(st_ref.shape[1:], F32)
    common = (rank, q_ref, k_ref, v_ref, lr_ref, w2_ref, b2_ref, nw_ref, o_ref, st_ref, oacc_ref)

    def half(first):
        def body(i, carry):
            _gla_dir_step(i, 0, first, *common)
            _gla_dir_step(n - 1 - i, 1, first, *common)
            return carry
        return body

    lax.fori_loop(0, n // 2, half(True), 0)
    lax.fori_loop(n // 2, n, half(False), 0)
    if want_state:
        for s in range(2):
            sfin_ref[s] = st_ref[s].T


def _gla_call(proj_main, proj_small, w2, b2, norm_w, state, layer, row0, batch, seq, heads, dk, dv, rank):
    has_init = state is not None
    want_state = not has_init
    qk_w = heads * dk
    assert seq % (2 * CHUNK) == 0 and (2 * qk_w) % dv == 0
    in_specs = [pl.BlockSpec((seq, dk), lambda b, h: (row0 + b, h)),
                pl.BlockSpec((seq, dk), lambda b, h: (row0 + b, heads + h)),
                pl.BlockSpec((seq, dv), lambda b, h: (row0 + b, 2 * qk_w // dv + h)),
                pl.BlockSpec((seq, proj_small.shape[1]), lambda b, h: (row0 + b, 0)),
                pl.BlockSpec((2, rank, dk), lambda b, h: (0, 0, h)),
                pl.BlockSpec((2, 1, dk), lambda b, h: (0, 0, h)),
                pl.BlockSpec((1, dv), lambda b, h: (0, 0))]
    args = [proj_main, proj_main, proj_main, proj_small, w2, b2.reshape(2, 1, qk_w), norm_w.reshape(1, dv)]
    if has_init:
        in_specs.append(pl.BlockSpec((None, None, 2, None, dk, dv), lambda b, h: (b, layer, 0, h, 0, 0)))
        args.append(state)
    out_specs = [pl.BlockSpec((seq, dv), lambda b, h: (b, h))]
    out_shape = [SDS((batch * seq, heads * dv), BF16)]
    if want_state:
        out_specs.append(pl.BlockSpec((None, 2, None, dk, dv), lambda b, h: (b, 0, h, 0, 0)))
        out_shape.append(SDS((batch, 2, heads, dk, dv), F32))
    outs = pl.pallas_call(
        functools.partial(_gla_kernel, rank=rank, has_init=has_init, want_state=want_state),
        grid=(batch, heads),
        in_specs=in_specs, out_specs=out_specs, out_shape=out_shape,
        scratch_shapes=[pltpu.VMEM((2, dv, dk), F32), pltpu.VMEM((seq, dv), F32)],
        compiler_params=_params(("arbitrary", "arbitrary")),
        name="gla",
    )(*args)
    return outs if want_state else (outs[0], None)


def _conv_kernel(x_ref, w_ref, o_ref, *, width, n_qk_blocks, n_q_blocks, qscale):
    seq = x_ref.shape[0]
    n_rows = seq // width
    cb = pl.program_id(1)
    w = w_ref[...]
    ridx = lax.broadcasted_iota(jnp.int32, (width, LANES), 0)

    def body(r, carry):
        acc = jnp.zeros((width, LANES), F32)
        for dr in (-1, 0, 1):
            if n_rows == 1 and dr != 0:
                continue
            src = jnp.clip(r + dr, 0, n_rows - 1)
            inside = jnp.logical_and(r + dr >= 0, r + dr < n_rows)
            tile = x_ref[pl.ds(pl.multiple_of(src * width, width), width), :].astype(F32)
            tile = jnp.where(inside, tile, 0.0)
            for dc in (-1, 0, 1):
                tap = (dr + 1) * 3 + (dc + 1)
                if dc == 0:
                    sh = tile
                else:
                    sh = pltpu.roll(tile, (-dc) % width, 0)
                    edge = width - 1 if dc == 1 else 0
                    sh = jnp.where(ridx == edge, 0.0, sh)
                acc = acc + sh * w[tap:tap + 1, :]
        y = _silu(acc)
        yn = y * lax.rsqrt(jnp.sum(y * y, axis=-1, keepdims=True) + EPS)
        yn = yn * jnp.where(cb < n_q_blocks, qscale, 1.0)
        y = jnp.where(cb < n_qk_blocks, yn, y)
        o_ref[pl.ds(pl.multiple_of(r * width, width), width), :] = y.astype(o_ref.dtype)
        return carry

    lax.fori_loop(0, n_rows, body, 0)


def _conv_call(proj_main, conv_w, col0, row0, batch, seq, width, heads, dk):
    ch = conv_w.shape[-1]
    nb = ch // LANES
    assert dk == LANES and col0 % LANES == 0 and seq % width == 0
    return pl.pallas_call(
        functools.partial(_conv_kernel, width=width, n_qk_blocks=2 * heads, n_q_blocks=heads, qscale=dk ** -0.5),
        grid=(batch, nb),
        in_specs=[pl.BlockSpec((seq, LANES), lambda b, c: (row0 + b, col0 // LANES + c)),
                  pl.BlockSpec((9, LANES), lambda b, c: (0, c))],
        out_specs=pl.BlockSpec((seq, LANES), lambda b, c: (b, c)),
        out_shape=SDS((batch * seq, ch), BF16),
        compiler_params=_params(("arbitrary", "arbitrary")),
        name="gdn_conv",
    )(proj_main, conv_w.reshape(9, ch))


def _lane_pick(x, idx):
    lane = lax.broadcasted_iota(jnp.int32, x.shape, 1)
    return jnp.sum(jnp.where(lane == idx, x, 0.0), axis=-1, keepdims=True)


def _gdn_dir_step(c, s, hh, first, head, a_off, q_ref, k_ref, v_ref, lr_ref, alog_ref, dtb_ref, nw_ref, o_ref,
                  st_ref, oacc_ref):
    rev = s == 1
    d = LANES
    n_heads = alog_ref.shape[1]
    rows = pl.ds(pl.multiple_of(c * CHUNK, CHUNK), CHUNK)
    cols = slice(hh * d, (hh + 1) * d)
    q = q_ref[rows, cols]
    k = k_ref[rows, cols]
    kf = k.astype(F32)
    v = v_ref[rows, cols].astype(F32)
    lr = lr_ref[rows, :]
    a = _lane_pick(lr, a_off + s * n_heads + head)
    b = _lane_pick(lr, a_off + (2 + s) * n_heads + head)
    neg_a = -jnp.exp(jnp.full((1, 1), alog_ref[s, head], F32))
    g = neg_a * _softplus(a + dtb_ref[s, head])
    beta = _sigmoid(b)
    incl, strict = _tri_masks(rev)
    tri = incl.astype(BF16)
    gb = jnp.broadcast_to(g, (CHUNK, d))
    cum = _tri_dot(tri, gb)
    dmat = _tri_dot(tri, jnp.where(strict, gb[:, :CHUNK], 0.0))
    gamma = jnp.where(incl, jnp.exp(jnp.where(incl, dmat, 0.0)), 0.0)
    qkk = lax.dot_general(jnp.concatenate([q, k], axis=0), k, _NT, preferred_element_type=F32)
    qk = qkk[:CHUNK] * gamma
    m = jnp.where(strict, beta * qkk[CHUNK:] * gamma, 0.0)
    mb = m.astype(BF16)
    y = -m
    p = jnp.dot(mb, mb, preferred_element_type=F32)
    steps = (CHUNK - 1).bit_length() - 1
    for it in range(steps):
        pb = p.astype(BF16)
        if it + 1 < steps:
            yp = jnp.dot(jnp.concatenate([y.astype(BF16), pb], axis=0), pb, preferred_element_type=F32)
            y = y + p + yp[:CHUNK]
            p = yp[CHUNK:]
        else:
            y = y + p + jnp.dot(y.astype(BF16), pb, preferred_element_type=F32)
    rhs = jnp.concatenate([v * beta, kf * (beta * jnp.exp(cum))], axis=1)
    sol = rhs + jnp.dot(y.astype(BF16), rhs.astype(BF16), preferred_element_type=F32)
    u = sol[:, :d]
    w = sol[:, d:]
    last_i = 0 if rev else CHUNK - 1
    clast = cum[last_i:last_i + 1, :]
    qd = q.astype(F32) * jnp.exp(cum)
    kd = kf * jnp.exp(clast - cum)
    st = st_ref[s, hh]
    wq = jnp.dot(jnp.concatenate([w.astype(BF16), qd.astype(BF16)], axis=0), st.astype(BF16),
                 preferred_element_type=F32)
    v_new = u - wq[:CHUNK]
    vb = v_new.astype(BF16)
    o = wq[CHUNK:] + jnp.dot(qk.astype(BF16), vb, preferred_element_type=F32)
    st_ref[s, hh] = st * jnp.exp(clast) + lax.dot_general(kd.astype(BF16), vb, _TN, preferred_element_type=F32)
    if first:
        oacc_ref[rows, cols] = o
    else:
        tot = oacc_ref[rows, cols] + o
        yn = tot * lax.rsqrt(jnp.mean(tot * tot, axis=-1, keepdims=True) + EPS) * nw_ref[...]
        o_ref[rows, cols] = yn.astype(o_ref.dtype)


def _gdn_kernel(*refs, hg, a_off, has_init, want_state):
    q_ref, k_ref, v_ref, lr_ref, alog_ref, dtb_ref, nw_ref = refs[:7]
    refs = refs[7:]
    if has_init:
        s0_ref, refs = refs[0], refs[1:]
    o_ref, refs = refs[0], refs[1:]
    if want_state:
        sfin_ref, refs = refs[0], refs[1:]
    st_ref, oacc_ref = refs
    n = q_ref.shape[0] // CHUNK
    if has_init:
        st_ref[...] = s0_ref[...]
    else:
        st_ref[...] = jnp.zeros(st_ref.shape, F32)
    grp = pl.program_id(1)
    common = (a_off, q_ref, k_ref, v_ref, lr_ref, alog_ref, dtb_ref, nw_ref, o_ref, st_ref, oacc_ref)

    def half(first):
        def body(i, carry):
            for hh in range(hg):
                _gdn_dir_step(i, 0, hh, first, grp * hg + hh, *common)
                _gdn_dir_step(n - 1 - i, 1, hh, first, grp * hg + hh, *common)
            return carry
        return body

    lax.fori_loop(0, n // 2, half(True), 0)
    lax.fori_loop(n // 2, n, half(False), 0)
    if want_state:
        sfin_ref[...] = st_ref[...]


def _gdn_call(conv, proj_small, a_log, dt_bias, norm_w, state, layer, row0, batch, seq, heads, hg, a_off):
    has_init = state is not None
    want_state = not has_init
    d = LANES
    ng = heads // hg
    assert heads % hg == 0 and seq % (2 * CHUNK) == 0
    smem = pl.BlockSpec(memory_space=pltpu.SMEM)
    in_specs = [pl.BlockSpec((seq, hg * d), lambda b, g: (b, g)),
                pl.BlockSpec((seq, hg * d), lambda b, g: (b, ng + g)),
                pl.BlockSpec((seq, hg * d), lambda b, g: (b, 2 * ng + g)),
                pl.BlockSpec((seq, proj_small.shape[1]), lambda b, g: (row0 + b, 0)),
                smem, smem,
                pl.BlockSpec((1, d), lambda b, g: (0, 0))]
    args = [conv, conv, conv, proj_small, a_log, dt_bias, norm_w.reshape(1, d)]
    if has_init:
        in_specs.append(pl.BlockSpec((None, None, 2, hg, d, d), lambda b, g: (b, layer, 0, g, 0, 0)))
        args.append(state)
    out_specs = [pl.BlockSpec((seq, hg * d), lambda b, g: (b, g))]
    out_shape = [SDS((batch * seq, heads * d), BF16)]
    if want_state:
        out_specs.append(pl.BlockSpec((None, 2, hg, d, d), lambda b, g: (b, 0, g, 0, 0)))
        out_shape.append(SDS((batch, 2, heads, d, d), F32))
    outs = pl.pallas_call(
        functools.partial(_gdn_kernel, hg=hg, a_off=a_off, has_init=has_init, want_state=want_state),
        grid=(batch, ng),
        in_specs=in_specs, out_specs=out_specs, out_shape=out_shape,
        scratch_shapes=[pltpu.VMEM((2, hg, d, d), F32), pltpu.VMEM((seq, hg * d), F32)],
        compiler_params=_params(("arbitrary", "arbitrary")),
        name="gdn",
    )(*args)
    return outs if want_state else (outs[0], None)


def _out_proj_kernel(gl_ref, gc_ref, dl_ref, dc_ref, gr_ref, dz_ref, x_ref, gate_ref, w_ref, o_ref, lhs_ref,
                     *, n_lat_blocks):
    i = pl.program_id(0)
    half = gl_ref.shape[1]

    def fill(g_ref, d_ref):
        lhs_ref[:, :half] = (g_ref[...].astype(F32) * _silu(gr_ref[...].astype(F32))).astype(BF16)
        lhs_ref[:, half:] = (d_ref[...].astype(F32) * _silu(dz_ref[...].astype(F32))).astype(BF16)

    @pl.when(jnp.logical_and(pl.program_id(1) == 0, i < n_lat_blocks))
    def _():
        fill(gl_ref, dl_ref)

    @pl.when(jnp.logical_and(pl.program_id(1) == 0, i >= n_lat_blocks))
    def _():
        fill(gc_ref, dc_ref)

    mix = jnp.dot(lhs_ref[...], w_ref[...], preferred_element_type=F32)
    o_ref[...] = x_ref[...] + gate_ref[...] * mix


def _out_proj_call(og_lat, og_ctx, od_lat, od_ctx, proj_main, gr_col, dz_col, x, mod3, w_out, mod_row, tm, tn):
    t, d = x.shape
    half = og_lat.shape[1]
    nl = og_lat.shape[0] // tm
    nc = og_ctx.shape[0] // tm
    assert gr_col % half == 0 and dz_col % half == 0 and nl * tm == og_lat.shape[0] and nc * tm == og_ctx.shape[0]
    lat_map = lambda i, j: (jnp.minimum(i, nl - 1), 0)
    ctx_map = lambda i, j: (jnp.clip(i - nl, 0, nc - 1), 0)
    return pl.pallas_call(
        functools.partial(_out_proj_kernel, n_lat_blocks=nl),
        grid=(t // tm, d // tn),
        in_specs=[pl.BlockSpec((tm, half), lat_map), pl.BlockSpec((tm, half), ctx_map),
                  pl.BlockSpec((tm, half), lat_map), pl.BlockSpec((tm, half), ctx_map),
                  pl.BlockSpec((tm, half), lambda i, j: (i, gr_col // half)),
                  pl.BlockSpec((tm, half), lambda i, j: (i, dz_col // half)),
                  pl.BlockSpec((tm, tn), lambda i, j: (i, j)),
                  pl.BlockSpec((None, 1, tn), lambda i, j: (mod_row(i, tm) * N_MOD + 2, 0, j)),
                  pl.BlockSpec((2 * half, tn), lambda i, j: (0, j))],
        out_specs=pl.BlockSpec((tm, tn), lambda i, j: (i, j)),
        out_shape=SDS((t, d), F32),
        scratch_shapes=[pltpu.VMEM((tm, 2 * half), BF16)],
        compiler_params=_params(("arbitrary", "arbitrary")),
        name="out_proj",
    )(og_lat, og_ctx, od_lat, od_ctx, proj_main, proj_main, x, mod3, w_out)


def _pick4(sel, vals):
    return jnp.where(sel == 0, vals[0], jnp.where(sel == 1, vals[1], jnp.where(sel == 2, vals[2], vals[3])))


def _router_kernel(x_ref, ln_ref, shift_ref, scale_ref, rw_ref, rb_ref, hp_ref, idx_ref, gate_ref, *, n_experts):
    x = x_ref[...]
    y = x * lax.rsqrt(jnp.mean(x * x, axis=-1, keepdims=True) + EPS) * ln_ref[...]
    h = y * (1.0 + scale_ref[...]) + shift_ref[...]
    hb = h.astype(BF16)
    half = h.shape[1] // 2
    bits = pltpu.bitcast(hb.astype(F32), jnp.uint32)
    hp_ref[...] = lax.shift_right_logical(bits[:, :half], jnp.uint32(16)) | bits[:, half:]
    h_hi, h_lo = hb, (h - hb.astype(F32)).astype(BF16)
    w_hi, w_lo = _split_bf16(rw_ref[...])
    logits = (lax.dot_general(w_hi, h_hi, _NT, preferred_element_type=F32)
              + lax.dot_general(w_lo, h_hi, _NT, preferred_element_type=F32)
              + lax.dot_general(w_hi, h_lo, _NT, preferred_element_type=F32))
    mx = jnp.max(logits, axis=0, keepdims=True)
    ex = jnp.exp(logits - mx)
    probs = ex / jnp.sum(ex, axis=0, keepdims=True)
    sel = probs + rb_ref[...]
    per = n_experts // N_GROUPS
    assert per == 4 and TOP_K == 2
    rows_s = [sel[e:e + 1, :] for e in range(n_experts)]
    rows_p = [probs[e:e + 1, :] for e in range(n_experts)]
    best = None
    best_score = None
    for g in range(N_GROUPS):
        r = rows_s[g * per:(g + 1) * per]
        score = None
        for a in range(per):
            for b in range(a + 1, per):
                pair = r[a] + r[b]
                score = pair if score is None else jnp.maximum(score, pair)
        if g == 0:
            best, best_score = jnp.zeros_like(score, dtype=jnp.int32), score
        else:
            better = score > best_score
            best = jnp.where(better, g, best)
            best_score = jnp.where(better, score, best_score)
    sg = [_pick4(best, [rows_s[g * per + a] for g in range(N_GROUPS)]) for a in range(per)]
    pg = [_pick4(best, [rows_p[g * per + a] for g in range(N_GROUPS)]) for a in range(per)]
    i1 = jnp.zeros_like(best)
    m1 = sg[0]
    for a in range(1, per):
        better = sg[a] > m1
        i1 = jnp.where(better, a, i1)
        m1 = jnp.where(better, sg[a], m1)
    i2 = None
    m2 = None
    for a in range(per):
        cand = jnp.where(i1 == a, -jnp.inf, sg[a])
        if i2 is None:
            i2, m2 = jnp.zeros_like(best), cand
        else:
            better = cand > m2
            i2 = jnp.where(better, a, i2)
            m2 = jnp.where(better, cand, m2)
    p1 = _pick4(i1, pg)
    p2 = _pick4(i2, pg)
    tot = p1 + p2
    idx_ref[0:1, :] = best * per + i1
    idx_ref[1:2, :] = best * per + i2
    gate_ref[0:1, :] = p1 / tot
    gate_ref[1:2, :] = p2 / tot


def _router_call(x, ln_w, mod3, router_w, router_bias, mod_row, tm):
    t, d = x.shape
    e = router_w.shape[1]
    return pl.pallas_call(
        functools.partial(_router_kernel, n_experts=e),
        grid=(t // tm,),
        in_specs=[pl.BlockSpec((tm, d), lambda i: (i, 0)),
                  pl.BlockSpec((1, d), lambda i: (0, 0)),
                  pl.BlockSpec((None, 1, d), lambda i: (mod_row(i, tm) * N_MOD + 3, 0, 0)),
                  pl.BlockSpec((None, 1, d), lambda i: (mod_row(i, tm) * N_MOD + 4, 0, 0)),
                  pl.BlockSpec((e, d), lambda i: (0, 0)),
                  pl.BlockSpec((e, 1), lambda i: (0, 0))],
        out_specs=[pl.BlockSpec((tm, d // 2), lambda i: (i, 0)),
                   pl.BlockSpec((TOP_K, tm), lambda i: (0, i)),
                   pl.BlockSpec((TOP_K, tm), lambda i: (0, i))],
        out_shape=[SDS((t, d // 2), jnp.uint32), SDS((TOP_K, t), jnp.int32), SDS((TOP_K, t), F32)],
        compiler_params=_params(("arbitrary",)),
        name="router",
    )(x, ln_w.reshape(1, d), mod3, mod3, router_w.T, router_bias.reshape(e, 1))


def _gather_kernel(nused_ref, tok_ref, src_ref, o_ref, sem):
    i = pl.program_id(0)
    rows = o_ref.shape[0]

    def copy(r):
        return pltpu.make_async_copy(src_ref.at[pl.ds(tok_ref[0, r], 1), :], o_ref.at[pl.ds(r, 1), :], sem)

    @pl.when(i < nused_ref[0])
    def _():
        def start(r, carry):
            copy(r).start()
            return carry

        def wait(r, carry):
            copy(r).wait()
            return carry

        lax.fori_loop(0, rows, start, 0)
        lax.fori_loop(0, rows, wait, 0)


def _gather_call(slot_tok, nused, src, rows):
    nb = slot_tok.shape[0] // rows
    width = src.shape[1]
    last = lambda i, nu: jnp.minimum(i, nu[0] - 1)
    return pl.pallas_call(
        _gather_kernel,
        grid_spec=pltpu.PrefetchScalarGridSpec(
            num_scalar_prefetch=1,
            grid=(nb,),
            in_specs=[pl.BlockSpec((None, 1, rows), lambda i, nu: (i, 0, 0), memory_space=pltpu.SMEM),
                      pl.BlockSpec(memory_space=pl.ANY)],
            out_specs=pl.BlockSpec((rows, width), lambda i, nu: (last(i, nu), 0)),
            scratch_shapes=[pltpu.SemaphoreType.DMA(())]),
        out_shape=SDS((nb * rows, width), src.dtype),
        compiler_params=_params(("arbitrary",)),
        name="moe_gather",
    )(nused, slot_tok.reshape(nb, 1, rows), src)


def _ffn_kernel(be_ref, nused_ref, xp_ref, w1_ref, w3_ref, w2_ref, g_ref, o_ref, xb_ref, act_ref):
    i = pl.program_id(0)
    s = pl.program_id(1)
    n_up, _, tf = act_ref.shape
    used = i < nused_ref[0]

    @pl.when(jnp.logical_and(used, s == 0))
    def _():
        words = xp_ref[...]
        half = words.shape[1]
        lo = pltpu.bitcast(lax.shift_left(words, jnp.uint32(16)), F32)
        hi = pltpu.bitcast(words & jnp.uint32(0xFFFF0000), F32)
        xb_ref[:, :half] = lo.astype(BF16)
        xb_ref[:, half:] = hi.astype(BF16)

    @pl.when(jnp.logical_and(used, s < n_up))
    def _():
        xb = xb_ref[...]
        a = jnp.dot(xb, w1_ref[...], preferred_element_type=F32)
        b = jnp.dot(xb, w3_ref[...], preferred_element_type=F32)
        act_ref[s] = (_silu(a) * b).astype(BF16)

    @pl.when(jnp.logical_and(used, s >= n_up))
    def _():
        acc = jnp.dot(act_ref[0], w2_ref[0:tf, :], preferred_element_type=F32)
        for f in range(1, n_up):
            acc = acc + jnp.dot(act_ref[f], w2_ref[f * tf:(f + 1) * tf, :], preferred_element_type=F32)
        o_ref[...] = acc * g_ref[...]


def _ffn_call(block_e, nused, xs, w1, w3, w2, slot_gate, rows, tf, tn):
    n_e, d, dff = w1.shape
    nb = xs.shape[0] // rows
    n_up, n_down = dff // tf, d // tn

    def blk(i, nu):
        return jnp.minimum(i, nu[0] - 1)

    def up_map(i, s, be, nu):
        return (be[blk(i, nu)], 0, jnp.where(i < nu[0], jnp.minimum(s, n_up - 1), n_up - 1))

    def down_tile(i, s, nu):
        return jnp.where(i < nu[0], jnp.maximum(s - n_up, 0), n_down - 1)

    return pl.pallas_call(
        _ffn_kernel,
        grid_spec=pltpu.PrefetchScalarGridSpec(
            num_scalar_prefetch=2,
            grid=(nb, n_up + n_down),
            in_specs=[pl.BlockSpec((rows, d // 2), lambda i, s, be, nu: (blk(i, nu), 0)),
                      pl.BlockSpec((None, d, tf), up_map),
                      pl.BlockSpec((None, d, tf), up_map),
                      pl.BlockSpec((None, dff, tn), lambda i, s, be, nu: (be[blk(i, nu)], 0, down_tile(i, s, nu))),
                      pl.BlockSpec((rows, 1), lambda i, s, be, nu: (blk(i, nu), 0))],
            out_specs=pl.BlockSpec((rows, tn), lambda i, s, be, nu: (blk(i, nu), down_tile(i, s, nu))),
            scratch_shapes=[pltpu.VMEM((rows, d), BF16), pltpu.VMEM((n_up, rows, tf), BF16)]),
        out_shape=SDS((nb * rows, d), F32),
        compiler_params=_params(("arbitrary", "arbitrary")),
        name="moe_ffn",
    )(block_e, nused, xs, w1, w3, w2, slot_gate.reshape(nb * rows, 1))


def _combine_kernel(pos_ref, x_ref, gate_ref, fw_ref, y_ref, o_ref, buf_ref, sem, *, final):
    tm = x_ref.shape[0]

    def copy(k, r):
        return pltpu.make_async_copy(y_ref.at[pl.ds(pos_ref[k, r], 1), :], buf_ref.at[k, pl.ds(r, 1), :], sem)

    def start(r, carry):
        for k in range(TOP_K):
            copy(k, r).start()
        return carry

    def wait(r, carry):
        for k in range(TOP_K):
            copy(k, r).wait()
        return carry

    lax.fori_loop(0, tm, start, 0)
    lax.fori_loop(0, tm, wait, 0)
    v = x_ref[...] + gate_ref[...] * (buf_ref[0] + buf_ref[1])
    if final:
        v = v * lax.rsqrt(jnp.mean(v * v, axis=-1, keepdims=True) + EPS) * fw_ref[...]
    o_ref[...] = v


def _combine_call(pos, x, mod3, final_w, ys, mod_row, tm, final):
    t, d = x.shape
    return pl.pallas_call(
        functools.partial(_combine_kernel, final=final),
        grid=(t // tm,),
        in_specs=[pl.BlockSpec((None, TOP_K, tm), lambda i: (i, 0, 0), memory_space=pltpu.SMEM),
                  pl.BlockSpec((tm, d), lambda i: (i, 0)),
                  pl.BlockSpec((None, 1, d), lambda i: (mod_row(i, tm) * N_MOD + 5, 0, 0)),
                  pl.BlockSpec((1, d), lambda i: (0, 0)),
                  pl.BlockSpec(memory_space=pl.ANY)],
        out_specs=pl.BlockSpec((tm, d), lambda i: (i, 0)),
        out_shape=SDS((t, d), F32),
        scratch_shapes=[pltpu.VMEM((TOP_K, tm, d), F32), pltpu.SemaphoreType.DMA(())],
        compiler_params=_params(("arbitrary",)),
        name="moe_combine",
    )(pos.reshape(TOP_K, t // tm, tm).transpose(1, 0, 2), x, mod3, final_w.reshape(1, d), ys)


def _dispatch_tables(idx, gate, n_experts, rows):
    t = idx.shape[1]
    n_assign = t * TOP_K
    flat_e = idx.T.reshape(-1)
    flat_g = gate.T.reshape(-1)
    order = jnp.argsort(flat_e, stable=True).astype(jnp.int32)
    counts = jnp.sum((flat_e[:, None] == jnp.arange(n_experts, dtype=jnp.int32)[None, :]).astype(jnp.int32), axis=0)
    padded = (counts + rows - 1) // rows * rows
    pad_end = jnp.cumsum(padded)
    pad_start = pad_end - padded
    start = jnp.cumsum(counts) - counts
    n_blocks = -(-n_assign // rows) + n_experts
    n_slots = n_blocks * rows
    block_e = jnp.minimum(jnp.searchsorted(pad_end, jnp.arange(n_blocks, dtype=jnp.int32) * rows, side='right'),
                          n_experts - 1).astype(jnp.int32)
    nused = (pad_end[-1] // rows).astype(jnp.int32).reshape(1)
    slot = jnp.arange(n_slots, dtype=jnp.int32)
    slot_e = jnp.repeat(block_e, rows)
    within = slot - pad_start[slot_e]
    valid = jnp.logical_and(within < counts[slot_e], slot < pad_end[-1])
    src = order[jnp.clip(start[slot_e] + within, 0, n_assign - 1)]
    slot_tok = jnp.where(valid, src // TOP_K, 0).astype(jnp.int32)
    slot_gate = jnp.where(valid, flat_g[src], 0.0)
    sorted_e = flat_e[order]
    dest = pad_start[sorted_e] + jnp.arange(n_assign, dtype=jnp.int32) - start[sorted_e]
    pos = jnp.zeros((n_assign,), jnp.int32).at[order].set(dest.astype(jnp.int32))
    return block_e, nused, slot_tok, slot_gate, pos.reshape(t, TOP_K).T


def _tile(n, target):
    if n <= target:
        return n
    best = None
    for cand in range(LANES, target + 1, LANES):
        if n % cand == 0:
            best = cand
    assert best is not None, (n, target)
    return best


def kernel(x_prompt, x_sample, state_gla, state_gdn, c, c_ctx, ln1_w, w_mod, b_mod, w_in, gla_w2, gla_b2, gla_norm_w, gdn_conv_w, gdn_a_log, gdn_dt_bias, gdn_norm_w, w_out, ln2_w, router_w, router_bias, w1, w3, w2, final_norm_w):
    bc, lc, d = x_prompt.shape
    bl, ll, _ = x_sample.shape
    depth = w_in.shape[0]
    gh, gdk, gdv = state_gla.shape[3:]
    dh, ddk, ddv = state_gdn.shape[3:]
    rank = gla_w2.shape[2]
    n_experts = router_w.shape[1]
    qk_w, d_gla, d_gdn = gh * gdk, gh * gdv, dh * ddv
    t_lat, t_ctx = bl * ll, bc * lc
    assert ddk == LANES and ddv == LANES and t_lat % lc == 0 and ll % GRID_W == 0

    x = jnp.concatenate([x_sample.reshape(t_lat, d), x_prompt.reshape(t_ctx, d)], axis=0)
    n_cond = -(-(bl + 1) // 8) * 8
    cond = jnp.concatenate([c, c_ctx[None, :], jnp.zeros((n_cond - bl - 1, d), F32)], axis=0)
    mod = _mod_call(cond, w_mod, b_mod)

    def mod_row(i, tm):
        return jnp.minimum(i * tm // ll, bl)

    o_glr = 2 * qk_w + 2 * d_gla
    o_dqkv = o_glr + 2 * rank
    o_da = o_dqkv + 4 * d_gdn
    n_small = 2 * rank + 4 * dh
    small_w = -(-n_small // LANES) * LANES
    gr_col = 2 * qk_w + d_gla
    dqkv_col = gr_col + d_gla
    dz_col = dqkv_col + 3 * d_gdn

    tm = math.gcd(math.gcd(ll, t_ctx), 512)
    gla_states, gdn_states = [], []
    for l in range(depth):
        mod3 = mod[l].reshape(n_cond * N_MOD, 1, d)
        wl = w_in[l]
        w_main = jnp.concatenate([wl[:, :o_glr], wl[:, o_dqkv:o_da]], axis=1).astype(BF16)
        w_small = jnp.concatenate([wl[:, o_glr:o_dqkv], wl[:, o_da:], jnp.zeros((d, small_w - n_small), F32)],
                                  axis=1).astype(BF16)
        proj_main, proj_small = _in_proj_call(x, ln1_w[l], mod3, w_main, w_small, mod_row, tm,
                                              _tile(w_main.shape[1], 1024))

        og_lat, _ = _gla_call(proj_main, proj_small, gla_w2[l], gla_b2[l], gla_norm_w[l], state_gla, l,
                              0, bl, ll, gh, gdk, gdv, rank)
        og_ctx, sg = _gla_call(proj_main, proj_small, gla_w2[l], gla_b2[l], gla_norm_w[l], None, l,
                               t_lat // lc, bc, lc, gh, gdk, gdv, rank)
        gla_states.append(sg)

        conv_lat = _conv_call(proj_main, gdn_conv_w[l], dqkv_col, 0, bl, ll, GRID_W, dh, ddk)
        conv_ctx = _conv_call(proj_main, gdn_conv_w[l], dqkv_col, t_lat // lc, bc, lc, lc, dh, ddk)
        hg = 2 if dh % 2 == 0 else 1
        od_lat, _ = _gdn_call(conv_lat, proj_small, gdn_a_log[l], gdn_dt_bias[l], gdn_norm_w[l], state_gdn, l,
                              0, bl, ll, dh, hg, 2 * rank)
        od_ctx, sd = _gdn_call(conv_ctx, proj_small, gdn_a_log[l], gdn_dt_bias[l], gdn_norm_w[l], None, l,
                               t_lat // lc, bc, lc, dh, hg, 2 * rank)
        gdn_states.append(sd)

        x1 = _out_proj_call(og_lat, og_ctx, od_lat, od_ctx, proj_main, gr_col, dz_col, x, mod3,
                            w_out[l].astype(BF16), mod_row, tm, _tile(d, 1024))

        tr = _tile(tm, 512)
        hp, idx, gate = _router_call(x1, ln2_w[l], mod3, router_w, router_bias, mod_row, tr)
        block_e, nused, slot_tok, slot_gate, pos = _dispatch_tables(idx, gate, n_experts, MOE_ROWS)
        xs = _gather_call(slot_tok, nused, hp, MOE_ROWS)
        ys = _ffn_call(block_e, nused, xs, w1[l].astype(BF16), w3[l].astype(BF16), w2[l].astype(BF16), slot_gate,
                       MOE_ROWS, _tile(w1.shape[-1], 512), _tile(d, 1024))
        x = _combine_call(pos, x1, mod3, final_norm_w, ys, mod_row, _tile(tm, 256), l == depth - 1)

    y_sample = x[:t_lat].reshape(bl, ll, d)
    y_prompt = x[t_lat:].reshape(bc, lc, d)
    return (y_prompt, y_sample, jnp.stack(gla_states, axis=1), jnp.stack(gdn_states, axis=1))
```

```python
import functools
import math

import jax
import jax.numpy as jnp
from jax import lax
from jax.experimental import pallas as pl
from jax.experimental.pallas import tpu as pltpu

F32 = jnp.float32
BF16 = jnp.bfloat16
SDS = jax.ShapeDtypeStruct

EPS = 1e-6
CHUNK = 64
GRID_W = 64
GLA_TAU = 16.0
N_GROUPS = 4
TOP_K = 2
N_MOD = 6
LANES = 128
MOE_ROWS = 512
VMEM_LIMIT = 56 * 1024 * 1024

_NT = (((1,), (1,)), ((), ()))
_TN = (((0,), (0,)), ((), ()))


def _params(sem, vmem=VMEM_LIMIT):
    return pltpu.CompilerParams(dimension_semantics=sem, vmem_limit_bytes=vmem)


def _sigmoid(x):
    return 1.0 / (1.0 + jnp.exp(-x))


def _silu(x):
    return x * _sigmoid(x)


def _softplus(x):
    return jnp.maximum(x, 0.0) + jnp.log1p(jnp.exp(-jnp.abs(x)))


def _split_bf16(x):
    hi = x.astype(BF16)
    lo = (x - hi.astype(F32)).astype(BF16)
    return hi, lo


def _tri_dot(tri, x):
    hi, lo = _split_bf16(x)
    return (jnp.dot(tri, hi, preferred_element_type=F32)
            + jnp.dot(tri, lo, preferred_element_type=F32))


def _tri_masks(rev):
    r = lax.broadcasted_iota(jnp.int32, (CHUNK, CHUNK), 0)
    c = lax.broadcasted_iota(jnp.int32, (CHUNK, CHUNK), 1)
    if rev:
        return r <= c, r < c
    return r >= c, r > c


def _mod_kernel(c_ref, w_ref, b_ref, o_ref):
    a = _silu(c_ref[...]).astype(BF16)
    o_ref[...] = jnp.dot(a, w_ref[...].astype(BF16), preferred_element_type=F32) + b_ref[...]


def _mod_call(cond, w_mod, b_mod):
    depth, d, n = w_mod.shape
    r = cond.shape[0]
    tn = 512
    return pl.pallas_call(
        _mod_kernel,
        grid=(depth, n // tn),
        in_specs=[pl.BlockSpec((r, d), lambda l, j: (0, 0)),
                  pl.BlockSpec((None, d, tn), lambda l, j: (l, 0, j)),
                  pl.BlockSpec((None, 1, tn), lambda l, j: (l, 0, j))],
        out_specs=pl.BlockSpec((None, r, tn), lambda l, j: (l, 0, j)),
        out_shape=SDS((depth, r, n), F32),
        compiler_params=_params(("arbitrary", "arbitrary")),
        name="mod",
    )(cond, w_mod, b_mod.reshape(depth, 1, n))


def _in_proj_kernel(x_ref, ln_ref, shift_ref, scale_ref, w_ref, ws_ref, o_ref, os_ref, h_ref):
    @pl.when(pl.program_id(1) == 0)
    def _():
        x = x_ref[...]
        y = x * lax.rsqrt(jnp.mean(x * x, axis=-1, keepdims=True) + EPS) * ln_ref[...]
        h = (y * (1.0 + scale_ref[...]) + shift_ref[...]).astype(BF16)
        h_ref[...] = h
        os_ref[...] = jnp.dot(h, ws_ref[...], preferred_element_type=F32)

    o_ref[...] = jnp.dot(h_ref[...], w_ref[...], preferred_element_type=F32).astype(o_ref.dtype)


def _in_proj_call(x, ln_w, mod3, w_main, w_small, mod_row, tm, tn):
    t, d = x.shape
    nm = w_main.shape[1]
    ns = w_small.shape[1]
    return pl.pallas_call(
        _in_proj_kernel,
        grid=(t // tm, nm // tn),
        in_specs=[pl.BlockSpec((tm, d), lambda i, j: (i, 0)),
                  pl.BlockSpec((1, d), lambda i, j: (0, 0)),
                  pl.BlockSpec((None, 1, d), lambda i, j: (mod_row(i, tm) * N_MOD + 0, 0, 0)),
                  pl.BlockSpec((None, 1, d), lambda i, j: (mod_row(i, tm) * N_MOD + 1, 0, 0)),
                  pl.BlockSpec((d, tn), lambda i, j: (0, j)),
                  pl.BlockSpec((d, ns), lambda i, j: (0, 0))],
        out_specs=[pl.BlockSpec((tm, tn), lambda i, j: (i, j)),
                   pl.BlockSpec((tm, ns), lambda i, j: (i, 0))],
        out_shape=[SDS((t, nm), BF16), SDS((t, ns), F32)],
        scratch_shapes=[pltpu.VMEM((tm, d), BF16)],
        compiler_params=_params(("arbitrary", "arbitrary")),
        name="in_proj",
    )(x, ln_w.reshape(1, d), mod3, mod3, w_main, w_small)


def _gla_iter(i, n, per, first, rank, q_ref, k_ref, v_ref, lr_ref, w2_ref, b2_ref, nw_ref, o_ref, st_ref, oacc_ref):
    dk = q_ref.shape[-1]
    jobs = [(s, (n - 1 - (i * per + u)) if s else (i * per + u)) for s in range(2) for u in range(per)]
    nj = len(jobs)
    rows = [pl.ds(pl.multiple_of(c * CHUNK, CHUNK), CHUNK) for s, c in jobs]
    incl = [_tri_masks(False)[0], _tri_masks(True)[0]]
    tri = [m.astype(BF16) for m in incl]
    w2 = [w2_ref[s].astype(BF16) for s in range(2)]
    q = [q_ref[rows[j], :].astype(F32) * dk ** -0.5 for j in range(nj)]
    k = [k_ref[rows[j], :].astype(F32) for j in range(nj)]
    v = [v_ref[rows[j], :] for j in range(nj)]
    lr = [lr_ref[rows[j], :][:, s * rank:(s + 1) * rank].astype(BF16) for j, (s, c) in enumerate(jobs)]
    pre = [jnp.dot(lr[j], w2[s], preferred_element_type=F32) + b2_ref[s] for j, (s, c) in enumerate(jobs)]
    g = [-_softplus(-x) / GLA_TAU for x in pre]
    cum = [_tri_dot(tri[s], g[j]) for j, (s, c) in enumerate(jobs)]
    cref = [cum[j][(CHUNK - 1 - CHUNK // 2 if s else CHUNK // 2):(CHUNK - CHUNK // 2 if s else CHUNK // 2 + 1), :]
            for j, (s, c) in enumerate(jobs)]
    clast = [cum[j][(0 if s else CHUNK - 1):(1 if s else CHUNK), :] for j, (s, c) in enumerate(jobs)]
    qa = [(q[j] * jnp.exp(cum[j] - cref[j])).astype(BF16) for j in range(nj)]
    ka = [(k[j] * jnp.exp(cref[j] - cum[j])).astype(BF16) for j in range(nj)]
    kd = [(k[j] * jnp.exp(clast[j] - cum[j])).astype(BF16) for j in range(nj)]
    qd = [(q[j] * jnp.exp(cum[j])).astype(BF16) for j in range(nj)]
    att = [lax.dot_general(qa[j], ka[j], _NT, preferred_element_type=F32) for j in range(nj)]
    kv = [lax.dot_general(v[j], kd[j], _TN, preferred_element_type=F32) for j in range(nj)]
    att = [jnp.where(incl[s], att[j], 0.0).astype(BF16) for j, (s, c) in enumerate(jobs)]
    o = [jnp.dot(att[j], v[j], preferred_element_type=F32) for j in range(nj)]
    st = [st_ref[0], st_ref[1]]
    for j, (s, c) in enumerate(jobs):
        o[j] = o[j] + lax.dot_general(qd[j], st[s].astype(BF16), _NT, preferred_element_type=F32)
        st[s] = st[s] * jnp.exp(clast[j]) + kv[j]
    for s in range(2):
        st_ref[s] = st[s]
    for j in range(nj):
        if first:
            oacc_ref[rows[j], :] = o[j]
        else:
            tot = oacc_ref[rows[j], :] + o[j]
            y = tot * lax.rsqrt(jnp.mean(tot * tot, axis=-1, keepdims=True) + EPS) * nw_ref[...]
            o_ref[rows[j], :] = y.astype(o_ref.dtype)


def _gla_kernel(*refs, rank, has_init, want_state):
    q_ref, k_ref, v_ref, lr_ref, w2_ref, b2_ref, nw_ref = refs[:7]
    refs = refs[7:]
    if has_init:
        s0_ref, refs = refs[0], refs[1:]
    o_ref, refs = refs[0], refs[1:]
    if want_state:
        sfin_ref, refs = refs[0], refs[1:]
    st_ref, oacc_ref = refs
    n = q_ref.shape[0] // CHUNK
    for s in range(2):
        if has_init:
            st_ref[s] = s0_ref[s].T
        else:
            st_ref[s] = jnp.zeros(st_ref.shape[1:], F32)
    common = (rank, q_ref, k_ref, v_ref, lr_ref, w2_ref, b2_ref, nw_ref, o_ref, st_ref, oacc_ref)
    per = 2 if n % 4 == 0 else 1

    def half(first):
        def body(i, carry):
            _gla_iter(i, n, per, first, *common)
            return carry
        return body

    lax.fori_loop(0, n // (2 * per), half(True), 0)
    lax.fori_loop(n // (2 * per), n // per, half(False), 0)
    if want_state:
        for s in range(2):
            sfin_ref[s] = st_ref[s].T


def _gla_call(proj_main, proj_small, w2, b2, norm_w, state, layer, row0, batch, seq, heads, dk, dv, rank):
    has_init = state is not None
    want_state = not has_init
    qk_w = heads * dk
    assert seq % (2 * CHUNK) == 0 and (2 * qk_w) % dv == 0
    in_specs = [pl.BlockSpec((seq, dk), lambda b, h: (row0 + b, h)),
                pl.BlockSpec((seq, dk), lambda b, h: (row0 + b, heads + h)),
                pl.BlockSpec((seq, dv), lambda b, h: (row0 + b, 2 * qk_w // dv + h)),
                pl.BlockSpec((seq, proj_small.shape[1]), lambda b, h: (row0 + b, 0)),
                pl.BlockSpec((2, rank, dk), lambda b, h: (0, 0, h)),
                pl.BlockSpec((2, 1, dk), lambda b, h: (0, 0, h)),
                pl.BlockSpec((1, dv), lambda b, h: (0, 0))]
    args = [proj_main, proj_main, proj_main, proj_small, w2, b2.reshape(2, 1, qk_w), norm_w.reshape(1, dv)]
    if has_init:
        in_specs.append(pl.BlockSpec((None, None, 2, None, dk, dv), lambda b, h: (b, layer, 0, h, 0, 0)))
        args.append(state)
    out_specs = [pl.BlockSpec((seq, dv), lambda b, h: (b, h))]
    out_shape = [SDS((batch * seq, heads * dv), BF16)]
    if want_state:
        out_specs.append(pl.BlockSpec((None, 2, None, dk, dv), lambda b, h: (b, 0, h, 0, 0)))
        out_shape.append(SDS((batch, 2, heads, dk, dv), F32))
    outs = pl.pallas_call(
        functools.partial(_gla_kernel, rank=rank, has_init=has_init, want_state=want_state),
        grid=(batch, heads),
        in_specs=in_specs, out_specs=out_specs, out_shape=out_shape,
        scratch_shapes=[pltpu.VMEM((2, dv, dk), F32), pltpu.VMEM((seq, dv), F32)],
        compiler_params=_params(("arbitrary", "arbitrary")),
        name="gla",
    )(*args)
    return outs if want_state else (outs[0], None)


def _conv_kernel(x_ref, w_ref, o_ref, xs_ref, *, width, n_qk_blocks, n_q_blocks, qscale):
    seq = x_ref.shape[0]
    n_rows = seq // width
    cb = pl.program_id(1)
    w = w_ref[...]
    ridx = lax.broadcasted_iota(jnp.int32, (width, LANES), 0)
    tile_rows = lambda r: pl.ds(pl.multiple_of(r * width, width), width)

    for t in range(3):
        xs_ref[t, tile_rows(0), :] = jnp.zeros((width, LANES), F32)
        xs_ref[t, tile_rows(n_rows + 1), :] = jnp.zeros((width, LANES), F32)

    def shift_body(r, carry):
        tile = x_ref[tile_rows(r), :].astype(F32)
        xs_ref[1, tile_rows(r + 1), :] = tile
        for dc in (-1, 1):
            sh = pltpu.roll(tile, (-dc) % width, 0)
            edge = width - 1 if dc == 1 else 0
            xs_ref[dc + 1, tile_rows(r + 1), :] = jnp.where(ridx == edge, 0.0, sh)
        return carry

    lax.fori_loop(0, n_rows, shift_body, 0, unroll=min(8, n_rows))

    def body(r, carry):
        acc = jnp.zeros((width, LANES), F32)
        for dr in (-1, 0, 1):
            if n_rows == 1 and dr != 0:
                continue
            for dc in (-1, 0, 1):
                tap = (dr + 1) * 3 + (dc + 1)
                acc = acc + xs_ref[dc + 1, tile_rows(r + dr + 1), :] * w[tap:tap + 1, :]
        y = _silu(acc)
        yn = y * lax.rsqrt(jnp.sum(y * y, axis=-1, keepdims=True) + EPS)
        yn = yn * jnp.where(cb < n_q_blocks, qscale, 1.0)
        y = jnp.where(cb < n_qk_blocks, yn, y)
        o_ref[tile_rows(r), :] = y.astype(o_ref.dtype)
        return carry

    lax.fori_loop(0, n_rows, body, 0, unroll=min(8, n_rows))


def _conv_call(proj_main, conv_w, col0, row0, batch, seq, width, heads, dk):
    ch = conv_w.shape[-1]
    nb = ch // LANES
    assert dk == LANES and col0 % LANES == 0 and seq % width == 0
    return pl.pallas_call(
        functools.partial(_conv_kernel, width=width, n_qk_blocks=2 * heads, n_q_blocks=heads, qscale=dk ** -0.5),
        grid=(batch, nb),
        in_specs=[pl.BlockSpec((seq, LANES), lambda b, c: (row0 + b, col0 // LANES + c)),
                  pl.BlockSpec((9, LANES), lambda b, c: (0, c))],
        out_specs=pl.BlockSpec((seq, LANES), lambda b, c: (b, c)),
        out_shape=SDS((batch * seq, ch), BF16),
        scratch_shapes=[pltpu.VMEM((3, seq + 2 * width, LANES), F32)],
        compiler_params=_params(("arbitrary", "arbitrary")),
        name="gdn_conv",
    )(proj_main, conv_w.reshape(9, ch))


def _gdn_iter(cf, cb, first, hg, n_ab, q_ref, k_ref, v_ref, lr_ref, arow_ref, dtrow_ref, nw_ref, o_ref, st_ref,
              oacc_ref, sel_ref):
    d = LANES
    chains = [(s, hh) for s in range(2) for hh in range(hg)]
    rows = [pl.ds(pl.multiple_of(c * CHUNK, CHUNK), CHUNK) for c in (cf, cb)]
    masks = [_tri_masks(False), _tri_masks(True)]
    tri = [m[0].astype(BF16) for m in masks]
    lane = lax.broadcasted_iota(jnp.int32, (CHUNK, d), 1)
    strict_w = [jnp.logical_and(lane < CHUNK, r_c) for r_c in
                (lax.broadcasted_iota(jnp.int32, (CHUNK, d), 0) > lane, lax.broadcasted_iota(jnp.int32, (CHUNK, d), 0) < lane)]

    tiles = []
    for s in range(2):
        lr = lr_ref[rows[s], :]
        g_t = -jnp.exp(arow_ref[...]) * _softplus(lr + dtrow_ref[...])
        tiles.append(_split_bf16(jnp.where(lane < n_ab, g_t, _sigmoid(lr))))
    q = [q_ref[rows[s], hh * d:(hh + 1) * d] for s, hh in chains]
    k = [k_ref[rows[s], hh * d:(hh + 1) * d] for s, hh in chains]
    v = [v_ref[rows[s], hh * d:(hh + 1) * d].astype(F32) for s, hh in chains]
    kf = [x.astype(F32) for x in k]

    sel = [jnp.dot(tiles[s][0], sel_ref[g], preferred_element_type=F32)
           + jnp.dot(tiles[s][1], sel_ref[g], preferred_element_type=F32) for g, (s, hh) in enumerate(chains)]
    qkk = [lax.dot_general(jnp.concatenate([q[g], k[g]], axis=0), k[g], _NT, preferred_element_type=F32)
           for g in range(len(chains))]
    gb = [x[:, :d] for x in sel]
    beta = [x[:, d:] for x in sel]
    tr = [_tri_dot(tri[s], jnp.concatenate([gb[g], jnp.where(strict_w[s], gb[g], 0.0)], axis=1))
          for g, (s, hh) in enumerate(chains)]
    cum = [x[:, :d] for x in tr]
    gamma = [jnp.where(masks[s][0], jnp.exp(jnp.where(masks[s][0], tr[g][:, d:d + CHUNK], 0.0)), 0.0)
             for g, (s, hh) in enumerate(chains)]
    qk = [qkk[g][:CHUNK] * gamma[g] for g in range(len(chains))]
    m = [jnp.where(masks[s][1], beta[g][:, :CHUNK] * qkk[g][CHUNK:] * gamma[g], 0.0)
         for g, (s, hh) in enumerate(chains)]
    y = [-x for x in m]
    mb = [x.astype(BF16) for x in m]
    p = [jnp.dot(x, x, preferred_element_type=F32) for x in mb]
    steps = (CHUNK - 1).bit_length() - 1
    for it in range(steps):
        pb = [x.astype(BF16) for x in p]
        if it + 1 < steps:
            yp = [jnp.dot(jnp.concatenate([y[g].astype(BF16), pb[g]], axis=0), pb[g], preferred_element_type=F32)
                  for g in range(len(chains))]
            y = [y[g] + p[g] + yp[g][:CHUNK] for g in range(len(chains))]
            p = [x[CHUNK:] for x in yp]
        else:
            yp = [jnp.dot(y[g].astype(BF16), pb[g], preferred_element_type=F32) for g in range(len(chains))]
            y = [y[g] + p[g] + yp[g] for g in range(len(chains))]
    rhs = [jnp.concatenate([v[g] * beta[g], kf[g] * (beta[g] * jnp.exp(cum[g]))], axis=1) for g in range(len(chains))]
    sol = [rhs[g] + jnp.dot(y[g].astype(BF16), rhs[g].astype(BF16), preferred_element_type=F32)
           for g in range(len(chains))]
    clast = [cum[g][(0 if s else CHUNK - 1):(1 if s else CHUNK), :] for g, (s, hh) in enumerate(chains)]
    qd = [q[g].astype(F32) * jnp.exp(cum[g]) for g in range(len(chains))]
    kd = [(kf[g] * jnp.exp(clast[g] - cum[g])).astype(BF16) for g in range(len(chains))]
    st = [st_ref[s, hh] for s, hh in chains]
    wq = [jnp.dot(jnp.concatenate([sol[g][:, d:].astype(BF16), qd[g].astype(BF16)], axis=0), st[g].astype(BF16),
                  preferred_element_type=F32) for g in range(len(chains))]
    vb = [(sol[g][:, :d] - wq[g][:CHUNK]).astype(BF16) for g in range(len(chains))]
    o = [wq[g][CHUNK:] + jnp.dot(qk[g].astype(BF16), vb[g], preferred_element_type=F32) for g in range(len(chains))]
    st_new = [st[g] * jnp.exp(clast[g]) + lax.dot_general(kd[g], vb[g], _TN, preferred_element_type=F32)
              for g in range(len(chains))]
    for g, (s, hh) in enumerate(chains):
        st_ref[s, hh] = st_new[g]
        cols = slice(hh * d, (hh + 1) * d)
        if first:
            oacc_ref[rows[s], cols] = o[g]
        else:
            tot = oacc_ref[rows[s], cols] + o[g]
            yn = tot * lax.rsqrt(jnp.mean(tot * tot, axis=-1, keepdims=True) + EPS) * nw_ref[...]
            o_ref[rows[s], cols] = yn.astype(o_ref.dtype)


def _gdn_kernel(*refs, hg, a_off, n_heads, has_init, want_state):
    q_ref, k_ref, v_ref, lr_ref, arow_ref, dtrow_ref, nw_ref = refs[:7]
    refs = refs[7:]
    if has_init:
        s0_ref, refs = refs[0], refs[1:]
    o_ref, refs = refs[0], refs[1:]
    if want_state:
        sfin_ref, refs = refs[0], refs[1:]
    st_ref, oacc_ref, sel_ref = refs
    n = q_ref.shape[0] // CHUNK
    if has_init:
        st_ref[...] = s0_ref[...]
    else:
        st_ref[...] = jnp.zeros(st_ref.shape, F32)
    grp = pl.program_id(1)
    r = lax.broadcasted_iota(jnp.int32, sel_ref.shape[1:], 0)
    c = lax.broadcasted_iota(jnp.int32, sel_ref.shape[1:], 1)
    for s in range(2):
        for hh in range(hg):
            head = grp * hg + hh
            col_a = a_off + s * n_heads + head
            col_b = a_off + (2 + s) * n_heads + head
            sel_ref[s * hg + hh] = jnp.where(r == jnp.where(c < LANES, col_a, col_b), 1.0, 0.0).astype(BF16)
    common = (hg, a_off + 2 * n_heads, q_ref, k_ref, v_ref, lr_ref, arow_ref, dtrow_ref, nw_ref, o_ref, st_ref,
              oacc_ref, sel_ref)

    def half(first):
        def body(i, carry):
            _gdn_iter(i, n - 1 - i, first, *common)
            return carry
        return body

    lax.fori_loop(0, n // 2, half(True), 0)
    lax.fori_loop(n // 2, n, half(False), 0)
    if want_state:
        sfin_ref[...] = st_ref[...]


def _gdn_call(conv, proj_small, a_log, dt_bias, norm_w, state, layer, row0, batch, seq, heads, hg, a_off):
    has_init = state is not None
    want_state = not has_init
    d = LANES
    ng = heads // hg
    small_w = proj_small.shape[1]
    assert heads % hg == 0 and seq % (2 * CHUNK) == 0 and small_w == LANES and a_off + 4 * heads <= small_w
    pad = lambda x: jnp.pad(x.reshape(1, 2 * heads), ((0, 0), (a_off, small_w - a_off - 2 * heads)))
    in_specs = [pl.BlockSpec((seq, hg * d), lambda b, g: (b, g)),
                pl.BlockSpec((seq, hg * d), lambda b, g: (b, ng + g)),
                pl.BlockSpec((seq, hg * d), lambda b, g: (b, 2 * ng + g)),
                pl.BlockSpec((seq, small_w), lambda b, g: (row0 + b, 0)),
                pl.BlockSpec((1, small_w), lambda b, g: (0, 0)),
                pl.BlockSpec((1, small_w), lambda b, g: (0, 0)),
                pl.BlockSpec((1, d), lambda b, g: (0, 0))]
    args = [conv, conv, conv, proj_small, pad(a_log), pad(dt_bias), norm_w.reshape(1, d)]
    if has_init:
        in_specs.append(pl.BlockSpec((None, None, 2, hg, d, d), lambda b, g: (b, layer, 0, g, 0, 0)))
        args.append(state)
    out_specs = [pl.BlockSpec((seq, hg * d), lambda b, g: (b, g))]
    out_shape = [SDS((batch * seq, heads * d), BF16)]
    if want_state:
        out_specs.append(pl.BlockSpec((None, 2, hg, d, d), lambda b, g: (b, 0, g, 0, 0)))
        out_shape.append(SDS((batch, 2, heads, d, d), F32))
    outs = pl.pallas_call(
        functools.partial(_gdn_kernel, hg=hg, a_off=a_off, n_heads=heads, has_init=has_init, want_state=want_state),
        grid=(batch, ng),
        in_specs=in_specs, out_specs=out_specs, out_shape=out_shape,
        scratch_shapes=[pltpu.VMEM((2, hg, d, d), F32), pltpu.VMEM((seq, hg * d), F32),
                        pltpu.VMEM((2 * hg, LANES, 2 * LANES), BF16)],
        compiler_params=_params(("arbitrary", "arbitrary")),
        name="gdn",
    )(*args)
    return outs if want_state else (outs[0], None)


def _out_proj_kernel(gl_ref, gc_ref, dl_ref, dc_ref, gr_ref, dz_ref, x_ref, gate_ref, w_ref, o_ref, lhs_ref,
                     *, n_lat_blocks):
    i = pl.program_id(0)
    half = gl_ref.shape[1]

    def fill(g_ref, d_ref):
        lhs_ref[:, :half] = (g_ref[...].astype(F32) * _silu(gr_ref[...].astype(F32))).astype(BF16)
        lhs_ref[:, half:] = (d_ref[...].astype(F32) * _silu(dz_ref[...].astype(F32))).astype(BF16)

    @pl.when(jnp.logical_and(pl.program_id(1) == 0, i < n_lat_blocks))
    def _():
        fill(gl_ref, dl_ref)

    @pl.when(jnp.logical_and(pl.program_id(1) == 0, i >= n_lat_blocks))
    def _():
        fill(gc_ref, dc_ref)

    mix = jnp.dot(lhs_ref[...], w_ref[...], preferred_element_type=F32)
    o_ref[...] = x_ref[...] + gate_ref[...] * mix


def _out_proj_call(og_lat, og_ctx, od_lat, od_ctx, proj_main, gr_col, dz_col, x, mod3, w_out, mod_row, tm, tn):
    t, d = x.shape
    half = og_lat.shape[1]
    nl = og_lat.shape[0] // tm
    nc = og_ctx.shape[0] // tm
    assert gr_col % half == 0 and dz_col % half == 0 and nl * tm == og_lat.shape[0] and nc * tm == og_ctx.shape[0]
    lat_map = lambda i, j: (jnp.minimum(i, nl - 1), 0)
    ctx_map = lambda i, j: (jnp.clip(i - nl, 0, nc - 1), 0)
    return pl.pallas_call(
        functools.partial(_out_proj_kernel, n_lat_blocks=nl),
        grid=(t // tm, d // tn),
        in_specs=[pl.BlockSpec((tm, half), lat_map), pl.BlockSpec((tm, half), ctx_map),
                  pl.BlockSpec((tm, half), lat_map), pl.BlockSpec((tm, half), ctx_map),
                  pl.BlockSpec((tm, half), lambda i, j: (i, gr_col // half)),
                  pl.BlockSpec((tm, half), lambda i, j: (i, dz_col // half)),
                  pl.BlockSpec((tm, tn), lambda i, j: (i, j)),
                  pl.BlockSpec((None, 1, tn), lambda i, j: (mod_row(i, tm) * N_MOD + 2, 0, j)),
                  pl.BlockSpec((2 * half, tn), lambda i, j: (0, j))],
        out_specs=pl.BlockSpec((tm, tn), lambda i, j: (i, j)),
        out_shape=SDS((t, d), F32),
        scratch_shapes=[pltpu.VMEM((tm, 2 * half), BF16)],
        compiler_params=_params(("arbitrary", "arbitrary")),
        name="out_proj",
    )(og_lat, og_ctx, od_lat, od_ctx, proj_main, proj_main, x, mod3, w_out)


def _pick4(sel, vals):
    return jnp.where(sel == 0, vals[0], jnp.where(sel == 1, vals[1], jnp.where(sel == 2, vals[2], vals[3])))


def _router_kernel(x_ref, ln_ref, shift_ref, scale_ref, rw_ref, rb_ref, hp_ref, idx_ref, gate_ref, *, n_experts):
    x = x_ref[...]
    y = x * lax.rsqrt(jnp.mean(x * x, axis=-1, keepdims=True) + EPS) * ln_ref[...]
    h = y * (1.0 + scale_ref[...]) + shift_ref[...]
    hb = h.astype(BF16)
    half = h.shape[1] // 2
    bits = pltpu.bitcast(hb.astype(F32), jnp.uint32)
    hp_ref[...] = lax.shift_right_logical(bits[:, :half], jnp.uint32(16)) | bits[:, half:]
    h_hi, h_lo = hb, (h - hb.astype(F32)).astype(BF16)
    w_hi, w_lo = _split_bf16(rw_ref[...])
    logits = (lax.dot_general(w_hi, h_hi, _NT, preferred_element_type=F32)
              + lax.dot_general(w_lo, h_hi, _NT, preferred_element_type=F32)
              + lax.dot_general(w_hi, h_lo, _NT, preferred_element_type=F32))
    mx = jnp.max(logits, axis=0, keepdims=True)
    ex = jnp.exp(logits - mx)
    probs = ex / jnp.sum(ex, axis=0, keepdims=True)
    sel = probs + rb_ref[...]
    per = n_experts // N_GROUPS
    assert per == 4 and TOP_K == 2
    rows_s = [sel[e:e + 1, :] for e in range(n_experts)]
    rows_p = [probs[e:e + 1, :] for e in range(n_experts)]
    best = None
    best_score = None
    for g in range(N_GROUPS):
        r = rows_s[g * per:(g + 1) * per]
        score = None
        for a in range(per):
            for b in range(a + 1, per):
                pair = r[a] + r[b]
                score = pair if score is None else jnp.maximum(score, pair)
        if g == 0:
            best, best_score = jnp.zeros_like(score, dtype=jnp.int32), score
        else:
            better = score > best_score
            best = jnp.where(better, g, best)
            best_score = jnp.where(better, score, best_score)
    sg = [_pick4(best, [rows_s[g * per + a] for g in range(N_GROUPS)]) for a in range(per)]
    pg = [_pick4(best, [rows_p[g * per + a] for g in range(N_GROUPS)]) for a in range(per)]
    i1 = jnp.zeros_like(best)
    m1 = sg[0]
    for a in range(1, per):
        better = sg[a] > m1
        i1 = jnp.where(better, a, i1)
        m1 = jnp.where(better, sg[a], m1)
    i2 = None
    m2 = None
    for a in range(per):
        cand = jnp.where(i1 == a, -jnp.inf, sg[a])
        if i2 is None:
            i2, m2 = jnp.zeros_like(best), cand
        else:
            better = cand > m2
            i2 = jnp.where(better, a, i2)
            m2 = jnp.where(better, cand, m2)
    p1 = _pick4(i1, pg)
    p2 = _pick4(i2, pg)
    tot = p1 + p2
    idx_ref[0:1, :] = best * per + i1
    idx_ref[1:2, :] = best * per + i2
    gate_ref[0:1, :] = p1 / tot
    gate_ref[1:2, :] = p2 / tot


def _router_call(x, ln_w, mod3, router_w, router_bias, mod_row, tm):
    t, d = x.shape
    e = router_w.shape[1]
    return pl.pallas_call(
        functools.partial(_router_kernel, n_experts=e),
        grid=(t // tm,),
        in_specs=[pl.BlockSpec((tm, d), lambda i: (i, 0)),
                  pl.BlockSpec((1, d), lambda i: (0, 0)),
                  pl.BlockSpec((None, 1, d), lambda i: (mod_row(i, tm) * N_MOD + 3, 0, 0)),
                  pl.BlockSpec((None, 1, d), lambda i: (mod_row(i, tm) * N_MOD + 4, 0, 0)),
                  pl.BlockSpec((e, d), lambda i: (0, 0)),
                  pl.BlockSpec((e, 1), lambda i: (0, 0))],
        out_specs=[pl.BlockSpec((tm, d // 2), lambda i: (i, 0)),
                   pl.BlockSpec((TOP_K, tm), lambda i: (0, i)),
                   pl.BlockSpec((TOP_K, tm), lambda i: (0, i))],
        out_shape=[SDS((t, d // 2), jnp.uint32), SDS((TOP_K, t), jnp.int32), SDS((TOP_K, t), F32)],
        compiler_params=_params(("arbitrary",)),
        name="router",
    )(x, ln_w.reshape(1, d), mod3, mod3, router_w.T, router_bias.reshape(e, 1))


def _gather_kernel(nused_ref, tok_ref, src_ref, o_ref, sem):
    i = pl.program_id(0)
    rows = o_ref.shape[0]

    def copy(r):
        return pltpu.make_async_copy(src_ref.at[pl.ds(tok_ref[0, r], 1), :], o_ref.at[pl.ds(r, 1), :], sem)

    @pl.when(i < nused_ref[0])
    def _():
        def start(r, carry):
            copy(r).start()
            return carry

        lax.fori_loop(0, rows, start, 0, unroll=8)
        pltpu.make_async_copy(src_ref.at[pl.ds(0, rows), :], o_ref, sem).wait()

    @pl.when(i >= nused_ref[0])
    def _():
        o_ref[...] = jnp.zeros(o_ref.shape, o_ref.dtype)


def _gather_call(slot_tok, nused, src, rows):
    nb = slot_tok.shape[0] // rows
    width = src.shape[1]
    return pl.pallas_call(
        _gather_kernel,
        grid_spec=pltpu.PrefetchScalarGridSpec(
            num_scalar_prefetch=1,
            grid=(nb,),
            in_specs=[pl.BlockSpec((None, 1, rows), lambda i, nu: (i, 0, 0), memory_space=pltpu.SMEM),
                      pl.BlockSpec(memory_space=pl.ANY)],
            out_specs=pl.BlockSpec((rows, width), lambda i, nu: (i, 0)),
            scratch_shapes=[pltpu.SemaphoreType.DMA(())]),
        out_shape=SDS((nb * rows, width), src.dtype),
        compiler_params=_params(("arbitrary",)),
        name="moe_gather",
    )(nused, slot_tok.reshape(nb, 1, rows), src)


def _ffn_kernel(be_ref, nused_ref, xp_ref, w1_ref, w3_ref, w2_ref, g_ref, o_ref, xb_ref, act_ref):
    i = pl.program_id(0)
    s = pl.program_id(1)
    n_up, _, tf = act_ref.shape
    used = i < nused_ref[0]

    @pl.when(jnp.logical_and(used, s == 0))
    def _():
        words = xp_ref[...]
        half = words.shape[1]
        lo = pltpu.bitcast(lax.shift_left(words, jnp.uint32(16)), F32)
        hi = pltpu.bitcast(words & jnp.uint32(0xFFFF0000), F32)
        xb_ref[:, :half] = lo.astype(BF16)
        xb_ref[:, half:] = hi.astype(BF16)

    @pl.when(jnp.logical_and(used, s < n_up))
    def _():
        xb = xb_ref[...]
        a = jnp.dot(xb, w1_ref[...], preferred_element_type=F32)
        b = jnp.dot(xb, w3_ref[...], preferred_element_type=F32)
        act_ref[s] = (_silu(a) * b).astype(BF16)

    @pl.when(jnp.logical_and(used, s >= n_up))
    def _():
        acc = jnp.dot(act_ref[0], w2_ref[0:tf, :], preferred_element_type=F32)
        for f in range(1, n_up):
            acc = acc + jnp.dot(act_ref[f], w2_ref[f * tf:(f + 1) * tf, :], preferred_element_type=F32)
        o_ref[...] = acc * g_ref[...]

    @pl.when(jnp.logical_and(jnp.logical_not(used), s >= n_up))
    def _():
        o_ref[...] = jnp.zeros(o_ref.shape, o_ref.dtype)


def _ffn_call(block_e, nused, xs, w1, w3, w2, slot_gate, rows, tf, tn):
    n_e, d, dff = w1.shape
    nb = xs.shape[0] // rows
    n_up, n_down = dff // tf, d // tn

    def blk(i, nu):
        return jnp.minimum(i, nu[0] - 1)

    def up_map(i, s, be, nu):
        return (be[blk(i, nu)], 0, jnp.where(i < nu[0], jnp.minimum(s, n_up - 1), n_up - 1))

    def down_tile(i, s, nu):
        return jnp.where(i < nu[0], jnp.maximum(s - n_up, 0), n_down - 1)

    return pl.pallas_call(
        _ffn_kernel,
        grid_spec=pltpu.PrefetchScalarGridSpec(
            num_scalar_prefetch=2,
            grid=(nb, n_up + n_down),
            in_specs=[pl.BlockSpec((rows, d // 2), lambda i, s, be, nu: (blk(i, nu), 0)),
                      pl.BlockSpec((None, d, tf), up_map),
                      pl.BlockSpec((None, d, tf), up_map),
                      pl.BlockSpec((None, dff, tn), lambda i, s, be, nu: (be[blk(i, nu)], 0, down_tile(i, s, nu))),
                      pl.BlockSpec((rows, 1), lambda i, s, be, nu: (blk(i, nu), 0))],
            out_specs=pl.BlockSpec((rows, tn), lambda i, s, be, nu: (i, jnp.maximum(s - n_up, 0))),
            scratch_shapes=[pltpu.VMEM((rows, d), BF16), pltpu.VMEM((n_up, rows, tf), BF16)]),
        out_shape=SDS((nb * rows, d), F32),
        compiler_params=_params(("arbitrary", "arbitrary")),
        name="moe_ffn",
    )(block_e, nused, xs, w1, w3, w2, slot_gate.reshape(nb * rows, 1))


def _combine_kernel(pos_ref, x_ref, gate_ref, fw_ref, y_ref, o_ref, buf_ref, sem, *, final):
    tm = x_ref.shape[0]

    def copy(k, r):
        return pltpu.make_async_copy(y_ref.at[pl.ds(pos_ref[k, r], 1), :], buf_ref.at[k, pl.ds(r, 1), :], sem)

    def start(r, carry):
        for k in range(TOP_K):
            copy(k, r).start()
        return carry

    lax.fori_loop(0, tm, start, 0, unroll=8)
    for k in range(TOP_K):
        pltpu.make_async_copy(y_ref.at[pl.ds(0, tm), :], buf_ref.at[k], sem).wait()
    v = x_ref[...] + gate_ref[...] * (buf_ref[0] + buf_ref[1])
    if final:
        v = v * lax.rsqrt(jnp.mean(v * v, axis=-1, keepdims=True) + EPS) * fw_ref[...]
    o_ref[...] = v


def _combine_call(pos, x, mod3, final_w, ys, mod_row, tm, final):
    t, d = x.shape
    return pl.pallas_call(
        functools.partial(_combine_kernel, final=final),
        grid=(t // tm,),
        in_specs=[pl.BlockSpec((None, TOP_K, tm), lambda i: (i, 0, 0), memory_space=pltpu.SMEM),
                  pl.BlockSpec((tm, d), lambda i: (i, 0)),
                  pl.BlockSpec((None, 1, d), lambda i: (mod_row(i, tm) * N_MOD + 5, 0, 0)),
                  pl.BlockSpec((1, d), lambda i: (0, 0)),
                  pl.BlockSpec(memory_space=pl.ANY)],
        out_specs=pl.BlockSpec((tm, d), lambda i: (i, 0)),
        out_shape=SDS((t, d), F32),
        scratch_shapes=[pltpu.VMEM((TOP_K, tm, d), F32), pltpu.SemaphoreType.DMA(())],
        compiler_params=_params(("arbitrary",)),
        name="moe_combine",
    )(pos.reshape(TOP_K, t // tm, tm).transpose(1, 0, 2), x, mod3, final_w.reshape(1, d), ys)


def _dispatch_tables(idx, gate, n_experts, rows):
    t = idx.shape[1]
    n_assign = t * TOP_K
    flat_e = idx.T.reshape(-1)
    flat_g = gate.T.reshape(-1)
    order = jnp.argsort(flat_e, stable=True).astype(jnp.int32)
    counts = jnp.sum((flat_e[:, None] == jnp.arange(n_experts, dtype=jnp.int32)[None, :]).astype(jnp.int32), axis=0)
    padded = (counts + rows - 1) // rows * rows
    pad_end = jnp.cumsum(padded)
    pad_start = pad_end - padded
    start = jnp.cumsum(counts) - counts
    n_blocks = -(-n_assign // rows) + n_experts
    n_slots = n_blocks * rows
    block_e = jnp.minimum(jnp.searchsorted(pad_end, jnp.arange(n_blocks, dtype=jnp.int32) * rows, side='right'),
                          n_experts - 1).astype(jnp.int32)
    nused = (pad_end[-1] // rows).astype(jnp.int32).reshape(1)
    slot = jnp.arange(n_slots, dtype=jnp.int32)
    slot_e = jnp.repeat(block_e, rows)
    within = slot - pad_start[slot_e]
    valid = jnp.logical_and(within < counts[slot_e], slot < pad_end[-1])
    src = order[jnp.clip(start[slot_e] + within, 0, n_assign - 1)]
    slot_tok = jnp.where(valid, src // TOP_K, 0).astype(jnp.int32)
    slot_gate = jnp.where(valid, flat_g[src], 0.0)
    sorted_e = flat_e[order]
    dest = pad_start[sorted_e] + jnp.arange(n_assign, dtype=jnp.int32) - start[sorted_e]
    pos = jnp.zeros((n_assign,), jnp.int32).at[order].set(dest.astype(jnp.int32))
    return block_e, nused, slot_tok, slot_gate, pos.reshape(t, TOP_K).T


def _tile(n, target):
    if n <= target:
        return n
    best = None
    for cand in range(LANES, target + 1, LANES):
        if n % cand == 0:
            best = cand
    assert best is not None, (n, target)
    return best


def kernel(x_prompt, x_sample, state_gla, state_gdn, c, c_ctx, ln1_w, w_mod, b_mod, w_in, gla_w2, gla_b2, gla_norm_w, gdn_conv_w, gdn_a_log, gdn_dt_bias, gdn_norm_w, w_out, ln2_w, router_w, router_bias, w1, w3, w2, final_norm_w):
    bc, lc, d = x_prompt.shape
    bl, ll, _ = x_sample.shape
    depth = w_in.shape[0]
    gh, gdk, gdv = state_gla.shape[3:]
    dh, ddk, ddv = state_gdn.shape[3:]
    rank = gla_w2.shape[2]
    n_experts = router_w.shape[1]
    qk_w, d_gla, d_gdn = gh * gdk, gh * gdv, dh * ddv
    t_lat, t_ctx = bl * ll, bc * lc
    assert ddk == LANES and ddv == LANES and t_lat % lc == 0 and ll % GRID_W == 0

    x = jnp.concatenate([x_sample.reshape(t_lat, d), x_prompt.reshape(t_ctx, d)], axis=0)
    n_cond = -(-(bl + 1) // 8) * 8
    cond = jnp.concatenate([c, c_ctx[None, :], jnp.zeros((n_cond - bl - 1, d), F32)], axis=0)
    mod = _mod_call(cond, w_mod, b_mod)

    def mod_row(i, tm):
        return jnp.minimum(i * tm // ll, bl)

    o_glr = 2 * qk_w + 2 * d_gla
    o_dqkv = o_glr + 2 * rank
    o_da = o_dqkv + 4 * d_gdn
    n_small = 2 * rank + 4 * dh
    small_w = -(-n_small // LANES) * LANES
    gr_col = 2 * qk_w + d_gla
    dqkv_col = gr_col + d_gla
    dz_col = dqkv_col + 3 * d_gdn

    tm = math.gcd(math.gcd(ll, t_ctx), 512)
    gla_states, gdn_states = [], []
    for l in range(depth):
        mod3 = mod[l].reshape(n_cond * N_MOD, 1, d)
        wl = w_in[l]
        w_main = jnp.concatenate([wl[:, :o_glr], wl[:, o_dqkv:o_da]], axis=1).astype(BF16)
        w_small = jnp.concatenate([wl[:, o_glr:o_dqkv], wl[:, o_da:], jnp.zeros((d, small_w - n_small), F32)],
                                  axis=1).astype(BF16)
        proj_main, proj_small = _in_proj_call(x, ln1_w[l], mod3, w_main, w_small, mod_row, tm,
                                              _tile(w_main.shape[1], 1024))

        og_lat, _ = _gla_call(proj_main, proj_small, gla_w2[l], gla_b2[l], gla_norm_w[l], state_gla, l,
                              0, bl, ll, gh, gdk, gdv, rank)
        og_ctx, sg = _gla_call(proj_main, proj_small, gla_w2[l], gla_b2[l], gla_norm_w[l], None, l,
                               t_lat // lc, bc, lc, gh, gdk, gdv, rank)
        gla_states.append(sg)

        conv_lat = _conv_call(proj_main, gdn_conv_w[l], dqkv_col, 0, bl, ll, GRID_W, dh, ddk)
        conv_ctx = _conv_call(proj_main, gdn_conv_w[l], dqkv_col, t_lat // lc, bc, lc, lc, dh, ddk)
        hg = math.gcd(dh, 4)
        od_lat, _ = _gdn_call(conv_lat, proj_small, gdn_a_log[l], gdn_dt_bias[l], gdn_norm_w[l], state_gdn, l,
                              0, bl, ll, dh, hg, 2 * rank)
        od_ctx, sd = _gdn_call(conv_ctx, proj_small, gdn_a_log[l], gdn_dt_bias[l], gdn_norm_w[l], None, l,
                               t_lat // lc, bc, lc, dh, hg, 2 * rank)
        gdn_states.append(sd)

        x1 = _out_proj_call(og_lat, og_ctx, od_lat, od_ctx, proj_main, gr_col, dz_col, x, mod3,
                            w_out[l].astype(BF16), mod_row, tm, _tile(d, 1024))

        tr = _tile(tm, 512)
        hp, idx, gate = _router_call(x1, ln2_w[l], mod3, router_w, router_bias, mod_row, tr)
        block_e, nused, slot_tok, slot_gate, pos = _dispatch_tables(idx, gate, n_experts, MOE_ROWS)
        xs = _gather_call(slot_tok, nused, hp, MOE_ROWS)
        ys = _ffn_call(block_e, nused, xs, w1[l].astype(BF16), w3[l].astype(BF16), w2[l].astype(BF16), slot_gate,
                       MOE_ROWS, _tile(w1.shape[-1], 512), _tile(d, 1024))
        x = _combine_call(pos, x1, mod3, final_norm_w, ys, mod_row, _tile(tm, 256), l == depth - 1)

    y_sample = x[:t_lat].reshape(bl, ll, d)
    y_prompt = x[t_lat:].reshape(bc, lc, d)
    return (y_prompt, y_sample, jnp.stack(gla_states, axis=1), jnp.stack(gdn_states, axis=1))
```

```python
import functools
import math

import jax
import jax.numpy as jnp
from jax import lax
from jax.experimental import pallas as pl
from jax.experimental.pallas import tpu as pltpu

F32 = jnp.float32
BF16 = jnp.bfloat16
SDS = jax.ShapeDtypeStruct

EPS = 1e-6
CHUNK = 64
GRID_W = 64
GLA_TAU = 16.0
N_GROUPS = 4
TOP_K = 2
N_MOD = 6
LANES = 128
MOE_ROWS = 512
VMEM_LIMIT = 56 * 1024 * 1024
VMEM_LIMIT_MAX = 60 * 1024 * 1024

_NT = (((1,), (1,)), ((), ()))
_TN = (((0,), (0,)), ((), ()))


def _params(sem, vmem=VMEM_LIMIT):
    return pltpu.CompilerParams(dimension_semantics=sem, vmem_limit_bytes=vmem)


def _sigmoid(x):
    return 1.0 / (1.0 + jnp.exp(-x))


def _silu(x):
    return x * _sigmoid(x)


def _softplus(x):
    return jnp.maximum(x, 0.0) + jnp.log1p(jnp.exp(-jnp.abs(x)))


def _split_bf16(x):
    hi = x.astype(BF16)
    lo = (x - hi.astype(F32)).astype(BF16)
    return hi, lo


def _tri_dot(tri, x):
    hi, lo = _split_bf16(x)
    return (jnp.dot(tri, hi, preferred_element_type=F32)
            + jnp.dot(tri, lo, preferred_element_type=F32))


def _tri_masks(rev):
    r = lax.broadcasted_iota(jnp.int32, (CHUNK, CHUNK), 0)
    c = lax.broadcasted_iota(jnp.int32, (CHUNK, CHUNK), 1)
    if rev:
        return r <= c, r < c
    return r >= c, r > c


def _mod_kernel(c_ref, w_ref, b_ref, o_ref):
    a = _silu(c_ref[...]).astype(BF16)
    o_ref[...] = jnp.dot(a, w_ref[...].astype(BF16), preferred_element_type=F32) + b_ref[...]


def _mod_call(cond, w_mod, b_mod):
    depth, d, n = w_mod.shape
    r = cond.shape[0]
    tn = 512
    return pl.pallas_call(
        _mod_kernel,
        grid=(depth, n // tn),
        in_specs=[pl.BlockSpec((r, d), lambda l, j: (0, 0)),
                  pl.BlockSpec((None, d, tn), lambda l, j: (l, 0, j)),
                  pl.BlockSpec((None, 1, tn), lambda l, j: (l, 0, j))],
        out_specs=pl.BlockSpec((None, r, tn), lambda l, j: (l, 0, j)),
        out_shape=SDS((depth, r, n), F32),
        compiler_params=_params(("arbitrary", "arbitrary")),
        name="mod",
    )(cond, w_mod, b_mod.reshape(depth, 1, n))


def _in_proj_kernel(x_ref, ln_ref, shift_ref, scale_ref, w_ref, ws_ref, o_ref, os_ref, h_ref):
    @pl.when(pl.program_id(1) == 0)
    def _():
        x = x_ref[...]
        y = x * lax.rsqrt(jnp.mean(x * x, axis=-1, keepdims=True) + EPS) * ln_ref[...]
        h = (y * (1.0 + scale_ref[...]) + shift_ref[...]).astype(BF16)
        h_ref[...] = h
        os_ref[...] = jnp.dot(h, ws_ref[...], preferred_element_type=F32)

    o_ref[...] = jnp.dot(h_ref[...], w_ref[...], preferred_element_type=F32).astype(o_ref.dtype)


def _in_proj_call(x, ln_w, mod3, w_main, w_small, layer, mod_row, tm, tn):
    t, d = x.shape
    nm = w_main.shape[2]
    ns = w_small.shape[2]
    return pl.pallas_call(
        _in_proj_kernel,
        grid=(t // tm, nm // tn),
        in_specs=[pl.BlockSpec((tm, d), lambda i, j: (i, 0)),
                  pl.BlockSpec((1, d), lambda i, j: (0, 0)),
                  pl.BlockSpec((None, 1, d), lambda i, j: (mod_row(i, tm) * N_MOD + 0, 0, 0)),
                  pl.BlockSpec((None, 1, d), lambda i, j: (mod_row(i, tm) * N_MOD + 1, 0, 0)),
                  pl.BlockSpec((None, d, tn), lambda i, j: (layer, 0, j)),
                  pl.BlockSpec((None, d, ns), lambda i, j: (layer, 0, 0))],
        out_specs=[pl.BlockSpec((tm, tn), lambda i, j: (i, j)),
                   pl.BlockSpec((tm, ns), lambda i, j: (i, 0))],
        out_shape=[SDS((t, nm), BF16), SDS((t, ns), F32)],
        scratch_shapes=[pltpu.VMEM((tm, d), BF16)],
        compiler_params=_params(("arbitrary", "arbitrary")),
        name="in_proj",
    )(x, ln_w.reshape(1, d), mod3, mod3, w_main, w_small)


def _gla_iter(i, n, per, first, rank, q_ref, k_ref, v_ref, lr_ref, w2_ref, b2_ref, nw_ref, o_ref, st_ref, oacc_ref):
    dk = q_ref.shape[-1]
    jobs = [(s, (n - 1 - (i * per + u)) if s else (i * per + u)) for s in range(2) for u in range(per)]
    nj = len(jobs)
    rows = [pl.ds(pl.multiple_of(c * CHUNK, CHUNK), CHUNK) for s, c in jobs]
    incl = [_tri_masks(False)[0], _tri_masks(True)[0]]
    tri = [m.astype(BF16) for m in incl]
    w2 = [w2_ref[s].astype(BF16) for s in range(2)]
    q = [q_ref[rows[j], :].astype(F32) * dk ** -0.5 for j in range(nj)]
    k = [k_ref[rows[j], :].astype(F32) for j in range(nj)]
    v = [v_ref[rows[j], :] for j in range(nj)]
    lr = [lr_ref[rows[j], :][:, s * rank:(s + 1) * rank].astype(BF16) for j, (s, c) in enumerate(jobs)]
    pre = [jnp.dot(lr[j], w2[s], preferred_element_type=F32) + b2_ref[s] for j, (s, c) in enumerate(jobs)]
    g = [-_softplus(-x) / GLA_TAU for x in pre]
    cum = [_tri_dot(tri[s], g[j]) for j, (s, c) in enumerate(jobs)]
    cref = [cum[j][(CHUNK - 1 - CHUNK // 2 if s else CHUNK // 2):(CHUNK - CHUNK // 2 if s else CHUNK // 2 + 1), :]
            for j, (s, c) in enumerate(jobs)]
    clast = [cum[j][(0 if s else CHUNK - 1):(1 if s else CHUNK), :] for j, (s, c) in enumerate(jobs)]
    qa = [(q[j] * jnp.exp(cum[j] - cref[j])).astype(BF16) for j in range(nj)]
    ka = [(k[j] * jnp.exp(cref[j] - cum[j])).astype(BF16) for j in range(nj)]
    kd = [(k[j] * jnp.exp(clast[j] - cum[j])).astype(BF16) for j in range(nj)]
    qd = [(q[j] * jnp.exp(cum[j])).astype(BF16) for j in range(nj)]
    att = [lax.dot_general(qa[j], ka[j], _NT, preferred_element_type=F32) for j in range(nj)]
    kv = [lax.dot_general(v[j], kd[j], _TN, preferred_element_type=F32) for j in range(nj)]
    att = [jnp.where(incl[s], att[j], 0.0).astype(BF16) for j, (s, c) in enumerate(jobs)]
    o = [jnp.dot(att[j], v[j], preferred_element_type=F32) for j in range(nj)]
    st = [st_ref[0], st_ref[1]]
    for j, (s, c) in enumerate(jobs):
        o[j] = o[j] + lax.dot_general(qd[j], st[s].astype(BF16), _NT, preferred_element_type=F32)
        st[s] = st[s] * jnp.exp(clast[j]) + kv[j]
    for s in range(2):
        st_ref[s] = st[s]
    for j in range(nj):
        if first:
            oacc_ref[rows[j], :] = o[j]
        else:
            tot = oacc_ref[rows[j], :] + o[j]
            y = tot * lax.rsqrt(jnp.mean(tot * tot, axis=-1, keepdims=True) + EPS) * nw_ref[...]
            o_ref[rows[j], :] = y.astype(o_ref.dtype)


def _gla_kernel(*refs, rank, has_init, want_state):
    q_ref, k_ref, v_ref, lr_ref, w2_ref, b2_ref, nw_ref = refs[:7]
    refs = refs[7:]
    if has_init:
        s0_ref, refs = refs[0], refs[1:]
    o_ref, refs = refs[0], refs[1:]
    if want_state:
        sfin_ref, refs = refs[0], refs[1:]
    st_ref, oacc_ref = refs
    n = q_ref.shape[0] // CHUNK
    for s in range(2):
        if has_init:
            st_ref[s] = s0_ref[s].T
        else:
            st_ref[s] = jnp.zeros(st_ref.shape[1:], F32)
    common = (rank, q_ref, k_ref, v_ref, lr_ref, w2_ref, b2_ref, nw_ref, o_ref, st_ref, oacc_ref)
    per = 2 if n % 4 == 0 else 1

    def half(first):
        def body(i, carry):
            _gla_iter(i, n, per, first, *common)
            return carry
        return body

    lax.fori_loop(0, n // (2 * per), half(True), 0)
    lax.fori_loop(n // (2 * per), n // per, half(False), 0)
    if want_state:
        for s in range(2):
            sfin_ref[s] = st_ref[s].T


def _gla_call(proj_main, proj_small, w2, b2, norm_w, state, layer, row0, batch, seq, heads, dk, dv, rank):
    has_init = state is not None
    want_state = not has_init
    qk_w = heads * dk
    assert seq % (2 * CHUNK) == 0 and (2 * qk_w) % dv == 0
    in_specs = [pl.BlockSpec((seq, dk), lambda b, h: (row0 + b, h)),
                pl.BlockSpec((seq, dk), lambda b, h: (row0 + b, heads + h)),
                pl.BlockSpec((seq, dv), lambda b, h: (row0 + b, 2 * qk_w // dv + h)),
                pl.BlockSpec((seq, proj_small.shape[1]), lambda b, h: (row0 + b, 0)),
                pl.BlockSpec((2, rank, dk), lambda b, h: (0, 0, h)),
                pl.BlockSpec((2, 1, dk), lambda b, h: (0, 0, h)),
                pl.BlockSpec((1, dv), lambda b, h: (0, 0))]
    args = [proj_main, proj_main, proj_main, proj_small, w2, b2.reshape(2, 1, qk_w), norm_w.reshape(1, dv)]
    if has_init:
        in_specs.append(pl.BlockSpec((None, None, 2, None, dk, dv), lambda b, h: (b, layer, 0, h, 0, 0)))
        args.append(state)
    out_specs = [pl.BlockSpec((seq, dv), lambda b, h: (b, h))]
    out_shape = [SDS((batch * seq, heads * dv), BF16)]
    if want_state:
        out_specs.append(pl.BlockSpec((None, 2, None, dk, dv), lambda b, h: (b, 0, h, 0, 0)))
        out_shape.append(SDS((batch, 2, heads, dk, dv), F32))
    outs = pl.pallas_call(
        functools.partial(_gla_kernel, rank=rank, has_init=has_init, want_state=want_state),
        grid=(batch, heads),
        in_specs=in_specs, out_specs=out_specs, out_shape=out_shape,
        scratch_shapes=[pltpu.VMEM((2, dv, dk), F32), pltpu.VMEM((seq, dv), F32)],
        compiler_params=_params(("arbitrary", "arbitrary")),
        name="gla",
    )(*args)
    return outs if want_state else (outs[0], None)


def _conv_kernel(x_ref, w_ref, o_ref, xs_ref, *, width, vertical, n_qk_blocks, n_q_blocks, qscale):
    seq = x_ref.shape[0]
    n_rows = seq // width
    cb = pl.program_id(1)
    w = w_ref[...]
    ridx = lax.broadcasted_iota(jnp.int32, (width, LANES), 0)
    tile_rows = lambda r: pl.ds(pl.multiple_of(r * width, width), width)

    for t in range(3):
        xs_ref[t, tile_rows(0), :] = jnp.zeros((width, LANES), F32)
        xs_ref[t, tile_rows(n_rows + 1), :] = jnp.zeros((width, LANES), F32)

    def shift_body(r, carry):
        tile = x_ref[tile_rows(r), :].astype(F32)
        xs_ref[1, tile_rows(r + 1), :] = tile
        for dc in (-1, 1):
            sh = pltpu.roll(tile, (-dc) % width, 0)
            edge = width - 1 if dc == 1 else 0
            xs_ref[dc + 1, tile_rows(r + 1), :] = jnp.where(ridx == edge, 0.0, sh)
        return carry

    unroll = max(1, min(n_rows, 8 * GRID_W // width))
    lax.fori_loop(0, n_rows, shift_body, 0, unroll=unroll)

    def body(r, carry):
        acc = jnp.zeros((width, LANES), F32)
        for dr in (-1, 0, 1):
            if not vertical and dr != 0:
                continue
            for dc in (-1, 0, 1):
                tap = (dr + 1) * 3 + (dc + 1)
                acc = acc + xs_ref[dc + 1, tile_rows(r + dr + 1), :] * w[tap:tap + 1, :]
        y = _silu(acc)
        yn = y * lax.rsqrt(jnp.sum(y * y, axis=-1, keepdims=True) + EPS)
        yn = yn * jnp.where(cb < n_q_blocks, qscale, 1.0)
        y = jnp.where(cb < n_qk_blocks, yn, y)
        o_ref[tile_rows(r), :] = y.astype(o_ref.dtype)
        return carry

    lax.fori_loop(0, n_rows, body, 0, unroll=unroll)


def _conv_call(proj_main, conv_w, col0, row0, batch, seq, width, vertical, heads, dk):
    ch = conv_w.shape[-1]
    nb = ch // LANES
    assert dk == LANES and col0 % LANES == 0 and seq % width == 0
    return pl.pallas_call(
        functools.partial(_conv_kernel, width=width, vertical=vertical, n_qk_blocks=2 * heads, n_q_blocks=heads,
                          qscale=dk ** -0.5),
        grid=(batch, nb),
        in_specs=[pl.BlockSpec((seq, LANES), lambda b, c: (row0 + b, col0 // LANES + c)),
                  pl.BlockSpec((9, LANES), lambda b, c: (0, c))],
        out_specs=pl.BlockSpec((seq, LANES), lambda b, c: (b, c)),
        out_shape=SDS((batch * seq, ch), BF16),
        scratch_shapes=[pltpu.VMEM((3, seq + 2 * width, LANES), F32)],
        compiler_params=_params(("arbitrary", "arbitrary")),
        name="gdn_conv",
    )(proj_main, conv_w.reshape(9, ch))


def _gdn_iter(cf, cb, first, hg, n_ab, q_ref, k_ref, v_ref, lr_ref, arow_ref, dtrow_ref, nw_ref, o_ref, st_ref,
              oacc_ref, sel_ref):
    d = LANES
    chains = [(s, hh) for s in range(2) for hh in range(hg)]
    rows = [pl.ds(pl.multiple_of(c * CHUNK, CHUNK), CHUNK) for c in (cf, cb)]
    masks = [_tri_masks(False), _tri_masks(True)]
    tri = [m[0].astype(BF16) for m in masks]
    lane = lax.broadcasted_iota(jnp.int32, (CHUNK, d), 1)
    strict_w = [jnp.logical_and(lane < CHUNK, r_c) for r_c in
                (lax.broadcasted_iota(jnp.int32, (CHUNK, d), 0) > lane, lax.broadcasted_iota(jnp.int32, (CHUNK, d), 0) < lane)]

    tiles = []
    for s in range(2):
        lr = lr_ref[rows[s], :]
        g_t = -jnp.exp(arow_ref[...]) * _softplus(lr + dtrow_ref[...])
        tiles.append(_split_bf16(jnp.where(lane < n_ab, g_t, _sigmoid(lr))))
    q = [q_ref[rows[s], hh * d:(hh + 1) * d] for s, hh in chains]
    k = [k_ref[rows[s], hh * d:(hh + 1) * d] for s, hh in chains]
    v = [v_ref[rows[s], hh * d:(hh + 1) * d].astype(F32) for s, hh in chains]
    kf = [x.astype(F32) for x in k]

    sel = [jnp.dot(tiles[s][0], sel_ref[g], preferred_element_type=F32)
           + jnp.dot(tiles[s][1], sel_ref[g], preferred_element_type=F32) for g, (s, hh) in enumerate(chains)]
    qkk = [lax.dot_general(jnp.concatenate([q[g], k[g]], axis=0), k[g], _NT, preferred_element_type=F32)
           for g in range(len(chains))]
    gb = [x[:, :d] for x in sel]
    beta = [x[:, d:] for x in sel]
    tr = [_tri_dot(tri[s], jnp.concatenate([gb[g], jnp.where(strict_w[s], gb[g], 0.0)], axis=1))
          for g, (s, hh) in enumerate(chains)]
    cum = [x[:, :d] for x in tr]
    gamma = [jnp.where(masks[s][0], jnp.exp(jnp.where(masks[s][0], tr[g][:, d:d + CHUNK], 0.0)), 0.0)
             for g, (s, hh) in enumerate(chains)]
    qk = [qkk[g][:CHUNK] * gamma[g] for g in range(len(chains))]
    m = [jnp.where(masks[s][1], beta[g][:, :CHUNK] * qkk[g][CHUNK:] * gamma[g], 0.0)
         for g, (s, hh) in enumerate(chains)]
    y = [-x for x in m]
    mb = [x.astype(BF16) for x in m]
    p = [jnp.dot(x, x, preferred_element_type=F32) for x in mb]
    steps = (CHUNK - 1).bit_length() - 1
    for it in range(steps):
        pb = [x.astype(BF16) for x in p]
        if it + 1 < steps:
            yp = [jnp.dot(jnp.concatenate([y[g].astype(BF16), pb[g]], axis=0), pb[g], preferred_element_type=F32)
                  for g in range(len(chains))]
            y = [y[g] + p[g] + yp[g][:CHUNK] for g in range(len(chains))]
            p = [x[CHUNK:] for x in yp]
        else:
            yp = [jnp.dot(y[g].astype(BF16), pb[g], preferred_element_type=F32) for g in range(len(chains))]
            y = [y[g] + p[g] + yp[g] for g in range(len(chains))]
    rhs = [jnp.concatenate([v[g] * beta[g], kf[g] * (beta[g] * jnp.exp(cum[g]))], axis=1) for g in range(len(chains))]
    sol = [rhs[g] + jnp.dot(y[g].astype(BF16), rhs[g].astype(BF16), preferred_element_type=F32)
           for g in range(len(chains))]
    clast = [cum[g][(0 if s else CHUNK - 1):(1 if s else CHUNK), :] for g, (s, hh) in enumerate(chains)]
    qd = [q[g].astype(F32) * jnp.exp(cum[g]) for g in range(len(chains))]
    kd = [(kf[g] * jnp.exp(clast[g] - cum[g])).astype(BF16) for g in range(len(chains))]
    st = [st_ref[s, hh] for s, hh in chains]
    wq = [jnp.dot(jnp.concatenate([sol[g][:, d:].astype(BF16), qd[g].astype(BF16)], axis=0), st[g].astype(BF16),
                  preferred_element_type=F32) for g in range(len(chains))]
    vb = [(sol[g][:, :d] - wq[g][:CHUNK]).astype(BF16) for g in range(len(chains))]
    o = [wq[g][CHUNK:] + jnp.dot(qk[g].astype(BF16), vb[g], preferred_element_type=F32) for g in range(len(chains))]
    st_new = [st[g] * jnp.exp(clast[g]) + lax.dot_general(kd[g], vb[g], _TN, preferred_element_type=F32)
              for g in range(len(chains))]
    for g, (s, hh) in enumerate(chains):
        st_ref[s, hh] = st_new[g]
        cols = slice(hh * d, (hh + 1) * d)
        if first:
            oacc_ref[rows[s], cols] = o[g]
        else:
            tot = oacc_ref[rows[s], cols] + o[g]
            yn = tot * lax.rsqrt(jnp.mean(tot * tot, axis=-1, keepdims=True) + EPS) * nw_ref[...]
            o_ref[rows[s], cols] = yn.astype(o_ref.dtype)


def _gdn_kernel(*refs, hg, a_off, n_heads, has_init, want_state):
    q_ref, k_ref, v_ref, lr_ref, arow_ref, dtrow_ref, nw_ref = refs[:7]
    refs = refs[7:]
    if has_init:
        s0_ref, refs = refs[0], refs[1:]
    o_ref, refs = refs[0], refs[1:]
    if want_state:
        sfin_ref, refs = refs[0], refs[1:]
    st_ref, oacc_ref, sel_ref = refs
    n = q_ref.shape[0] // CHUNK
    if has_init:
        st_ref[...] = s0_ref[...]
    else:
        st_ref[...] = jnp.zeros(st_ref.shape, F32)
    grp = pl.program_id(1)
    r = lax.broadcasted_iota(jnp.int32, sel_ref.shape[1:], 0)
    c = lax.broadcasted_iota(jnp.int32, sel_ref.shape[1:], 1)
    for s in range(2):
        for hh in range(hg):
            head = grp * hg + hh
            col_a = a_off + s * n_heads + head
            col_b = a_off + (2 + s) * n_heads + head
            sel_ref[s * hg + hh] = jnp.where(r == jnp.where(c < LANES, col_a, col_b), 1.0, 0.0).astype(BF16)
    common = (hg, a_off + 2 * n_heads, q_ref, k_ref, v_ref, lr_ref, arow_ref, dtrow_ref, nw_ref, o_ref, st_ref,
              oacc_ref, sel_ref)

    def half(first):
        def body(i, carry):
            _gdn_iter(i, n - 1 - i, first, *common)
            return carry
        return body

    lax.fori_loop(0, n // 2, half(True), 0)
    lax.fori_loop(n // 2, n, half(False), 0)
    if want_state:
        sfin_ref[...] = st_ref[...]


def _gdn_call(conv, proj_small, a_log, dt_bias, norm_w, state, layer, row0, batch, seq, heads, hg, a_off):
    has_init = state is not None
    want_state = not has_init
    d = LANES
    ng = heads // hg
    small_w = proj_small.shape[1]
    assert heads % hg == 0 and seq % (2 * CHUNK) == 0 and small_w == LANES and a_off + 4 * heads <= small_w
    pad = lambda x: jnp.pad(x.reshape(1, 2 * heads), ((0, 0), (a_off, small_w - a_off - 2 * heads)))
    wide = seq * hg * d
    fits = (3 * 2 * 2 + 2 * 2 + 4) * wide <= VMEM_LIMIT * 3 // 4
    one = {} if fits else dict(pipeline_mode=pl.Buffered(1))
    in_specs = [pl.BlockSpec((seq, hg * d), lambda b, g: (b, g), **one),
                pl.BlockSpec((seq, hg * d), lambda b, g: (b, ng + g), **one),
                pl.BlockSpec((seq, hg * d), lambda b, g: (b, 2 * ng + g), **one),
                pl.BlockSpec((seq, small_w), lambda b, g: (row0 + b, 0), **one),
                pl.BlockSpec((1, small_w), lambda b, g: (0, 0)),
                pl.BlockSpec((1, small_w), lambda b, g: (0, 0)),
                pl.BlockSpec((1, d), lambda b, g: (0, 0))]
    args = [conv, conv, conv, proj_small, pad(a_log), pad(dt_bias), norm_w.reshape(1, d)]
    if has_init:
        in_specs.append(pl.BlockSpec((None, None, 2, hg, d, d), lambda b, g: (b, layer, 0, g, 0, 0), **one))
        args.append(state)
    out_specs = [pl.BlockSpec((seq, hg * d), lambda b, g: (b, g), **one)]
    out_shape = [SDS((batch * seq, heads * d), BF16)]
    if want_state:
        out_specs.append(pl.BlockSpec((None, 2, hg, d, d), lambda b, g: (b, 0, g, 0, 0)))
        out_shape.append(SDS((batch, 2, heads, d, d), F32))
    outs = pl.pallas_call(
        functools.partial(_gdn_kernel, hg=hg, a_off=a_off, n_heads=heads, has_init=has_init, want_state=want_state),
        grid=(batch, ng),
        in_specs=in_specs, out_specs=out_specs, out_shape=out_shape,
        scratch_shapes=[pltpu.VMEM((2, hg, d, d), F32), pltpu.VMEM((seq, hg * d), F32),
                        pltpu.VMEM((2 * hg, LANES, 2 * LANES), BF16)],
        compiler_params=_params(("arbitrary", "arbitrary"), VMEM_LIMIT if fits else VMEM_LIMIT_MAX),
        name="gdn",
    )(*args)
    return outs if want_state else (outs[0], None)


def _out_proj_kernel(gl_ref, gc_ref, dl_ref, dc_ref, gr_ref, dz_ref, x_ref, gate_ref, w_ref, o_ref, lhs_ref,
                     *, n_lat_blocks):
    i = pl.program_id(0)
    half = gl_ref.shape[1]

    def fill(g_ref, d_ref):
        lhs_ref[:, :half] = (g_ref[...].astype(F32) * _silu(gr_ref[...].astype(F32))).astype(BF16)
        lhs_ref[:, half:] = (d_ref[...].astype(F32) * _silu(dz_ref[...].astype(F32))).astype(BF16)

    @pl.when(jnp.logical_and(pl.program_id(1) == 0, i < n_lat_blocks))
    def _():
        fill(gl_ref, dl_ref)

    @pl.when(jnp.logical_and(pl.program_id(1) == 0, i >= n_lat_blocks))
    def _():
        fill(gc_ref, dc_ref)

    mix = jnp.dot(lhs_ref[...], w_ref[...], preferred_element_type=F32)
    o_ref[...] = x_ref[...] + gate_ref[...] * mix


def _out_proj_call(og_lat, og_ctx, od_lat, od_ctx, proj_main, gr_col, dz_col, x, mod3, w_out, layer, mod_row, tm, tn):
    t, d = x.shape
    half = og_lat.shape[1]
    nl = og_lat.shape[0] // tm
    nc = og_ctx.shape[0] // tm
    assert gr_col % half == 0 and dz_col % half == 0 and nl * tm == og_lat.shape[0] and nc * tm == og_ctx.shape[0]
    lat_map = lambda i, j: (jnp.minimum(i, nl - 1), 0)
    ctx_map = lambda i, j: (jnp.clip(i - nl, 0, nc - 1), 0)
    return pl.pallas_call(
        functools.partial(_out_proj_kernel, n_lat_blocks=nl),
        grid=(t // tm, d // tn),
        in_specs=[pl.BlockSpec((tm, half), lat_map), pl.BlockSpec((tm, half), ctx_map),
                  pl.BlockSpec((tm, half), lat_map), pl.BlockSpec((tm, half), ctx_map),
                  pl.BlockSpec((tm, half), lambda i, j: (i, gr_col // half)),
                  pl.BlockSpec((tm, half), lambda i, j: (i, dz_col // half)),
                  pl.BlockSpec((tm, tn), lambda i, j: (i, j)),
                  pl.BlockSpec((None, 1, tn), lambda i, j: (mod_row(i, tm) * N_MOD + 2, 0, j)),
                  pl.BlockSpec((None, 2 * half, tn), lambda i, j: (layer, 0, j))],
        out_specs=pl.BlockSpec((tm, tn), lambda i, j: (i, j)),
        out_shape=SDS((t, d), F32),
        scratch_shapes=[pltpu.VMEM((tm, 2 * half), BF16)],
        compiler_params=_params(("arbitrary", "arbitrary")),
        name="out_proj",
    )(og_lat, og_ctx, od_lat, od_ctx, proj_main, proj_main, x, mod3, w_out)


def _pick4(sel, vals):
    return jnp.where(sel == 0, vals[0], jnp.where(sel == 1, vals[1], jnp.where(sel == 2, vals[2], vals[3])))


def _router_kernel(x_ref, ln_ref, shift_ref, scale_ref, rw_ref, rb_ref, hp_ref, idx_ref, gate_ref, *, n_experts):
    x = x_ref[...]
    y = x * lax.rsqrt(jnp.mean(x * x, axis=-1, keepdims=True) + EPS) * ln_ref[...]
    h = y * (1.0 + scale_ref[...]) + shift_ref[...]
    hb = h.astype(BF16)
    half = h.shape[1] // 2
    bits = pltpu.bitcast(hb.astype(F32), jnp.uint32)
    hp_ref[...] = lax.shift_right_logical(bits[:, :half], jnp.uint32(16)) | bits[:, half:]
    h_hi, h_lo = hb, (h - hb.astype(F32)).astype(BF16)
    w_hi, w_lo = _split_bf16(rw_ref[...])
    logits = (lax.dot_general(w_hi, h_hi, _NT, preferred_element_type=F32)
              + lax.dot_general(w_lo, h_hi, _NT, preferred_element_type=F32)
              + lax.dot_general(w_hi, h_lo, _NT, preferred_element_type=F32))
    mx = jnp.max(logits, axis=0, keepdims=True)
    ex = jnp.exp(logits - mx)
    probs = ex / jnp.sum(ex, axis=0, keepdims=True)
    sel = probs + rb_ref[...]
    per = n_experts // N_GROUPS
    assert per == 4 and TOP_K == 2
    rows_s = [sel[e:e + 1, :] for e in range(n_experts)]
    rows_p = [probs[e:e + 1, :] for e in range(n_experts)]
    best = None
    best_score = None
    for g in range(N_GROUPS):
        r = rows_s[g * per:(g + 1) * per]
        score = None
        for a in range(per):
            for b in range(a + 1, per):
                pair = r[a] + r[b]
                score = pair if score is None else jnp.maximum(score, pair)
        if g == 0:
            best, best_score = jnp.zeros_like(score, dtype=jnp.int32), score
        else:
            better = score > best_score
            best = jnp.where(better, g, best)
            best_score = jnp.where(better, score, best_score)
    sg = [_pick4(best, [rows_s[g * per + a] for g in range(N_GROUPS)]) for a in range(per)]
    pg = [_pick4(best, [rows_p[g * per + a] for g in range(N_GROUPS)]) for a in range(per)]
    i1 = jnp.zeros_like(best)
    m1 = sg[0]
    for a in range(1, per):
        better = sg[a] > m1
        i1 = jnp.where(better, a, i1)
        m1 = jnp.where(better, sg[a], m1)
    i2 = None
    m2 = None
    for a in range(per):
        cand = jnp.where(i1 == a, -jnp.inf, sg[a])
        if i2 is None:
            i2, m2 = jnp.zeros_like(best), cand
        else:
            better = cand > m2
            i2 = jnp.where(better, a, i2)
            m2 = jnp.where(better, cand, m2)
    p1 = _pick4(i1, pg)
    p2 = _pick4(i2, pg)
    tot = p1 + p2
    idx_ref[0:1, :] = best * per + i1
    idx_ref[1:2, :] = best * per + i2
    gate_ref[0:1, :] = p1 / tot
    gate_ref[1:2, :] = p2 / tot


def _router_call(x, ln_w, mod3, router_w, router_bias, mod_row, tm):
    t, d = x.shape
    e = router_w.shape[1]
    return pl.pallas_call(
        functools.partial(_router_kernel, n_experts=e),
        grid=(t // tm,),
        in_specs=[pl.BlockSpec((tm, d), lambda i: (i, 0)),
                  pl.BlockSpec((1, d), lambda i: (0, 0)),
                  pl.BlockSpec((None, 1, d), lambda i: (mod_row(i, tm) * N_MOD + 3, 0, 0)),
                  pl.BlockSpec((None, 1, d), lambda i: (mod_row(i, tm) * N_MOD + 4, 0, 0)),
                  pl.BlockSpec((e, d), lambda i: (0, 0)),
                  pl.BlockSpec((e, 1), lambda i: (0, 0))],
        out_specs=[pl.BlockSpec((tm, d // 2), lambda i: (i, 0)),
                   pl.BlockSpec((TOP_K, tm), lambda i: (0, i)),
                   pl.BlockSpec((TOP_K, tm), lambda i: (0, i))],
        out_shape=[SDS((t, d // 2), jnp.uint32), SDS((TOP_K, t), jnp.int32), SDS((TOP_K, t), F32)],
        compiler_params=_params(("arbitrary",)),
        name="router",
    )(x, ln_w.reshape(1, d), mod3, mod3, router_w.T, router_bias.reshape(e, 1))


def _moe_kernel(be_ref, nused_ref, tok0_ref, tokn_ref, dst_ref, hp_ref, w1_ref, w3_ref, w2_ref, g_ref, y_ref,
                xbuf_ref, xb_ref, act_ref, obuf_ref, gsem, ssem):
    i = pl.program_id(0)
    s = pl.program_id(1)
    n_steps = pl.num_programs(1)
    n_up, rows, tf = act_ref.shape
    tn = w2_ref.shape[1]
    n_down = obuf_ref.shape[1] // tn
    nused = nused_ref[0]
    used = i < nused
    slot = lax.rem(i, 2)
    g_per = rows // (n_up + n_down)
    s_per = rows // n_up

    def gather(tab_ref, r, sl):
        return pltpu.make_async_copy(hp_ref.at[pl.ds(tab_ref[0, r], 1), :], xbuf_ref.at[sl, pl.ds(r, 1), :], gsem.at[sl])

    def scatter(r):
        return pltpu.make_async_copy(obuf_ref.at[pl.ds(r, 1), :], y_ref.at[pl.ds(dst_ref[0, r], 1), :], ssem)

    def gather_wait(sl):
        pltpu.make_async_copy(hp_ref.at[pl.ds(0, rows), :], xbuf_ref.at[sl], gsem.at[sl]).wait()

    def scatter_wait():
        pltpu.make_async_copy(obuf_ref, y_ref.at[pl.ds(0, rows), :], ssem).wait()

    def gather_next():
        for r in range(g_per):
            gather(tokn_ref, s * g_per + r, 1 - slot).start()

    @pl.when(jnp.logical_and(i == 0, s == 0))
    def _():
        obuf_ref[...] = jnp.zeros(obuf_ref.shape, F32)

        def start(r, carry):
            gather(tok0_ref, r, 0).start()
            return carry

        lax.fori_loop(0, rows, start, 0, unroll=8)

    @pl.when(jnp.logical_and(used, s == 0))
    def _():
        gather_wait(slot)
        words = xbuf_ref[slot]
        half = words.shape[1]
        lo = pltpu.bitcast(lax.shift_left(words, jnp.uint32(16)), F32)
        hi = pltpu.bitcast(words & jnp.uint32(0xFFFF0000), F32)
        xb_ref[:, :half] = lo.astype(BF16)
        xb_ref[:, half:] = hi.astype(BF16)

    @pl.when(jnp.logical_and(used, s < n_up))
    def _():
        xb = xb_ref[...]
        a = jnp.dot(xb, w1_ref[...], preferred_element_type=F32)
        b = jnp.dot(xb, w3_ref[...], preferred_element_type=F32)
        act_ref[s] = (_silu(a) * b).astype(BF16)
        gather_next()
        for r in range(s_per):
            scatter(s * s_per + r).start()

    @pl.when(jnp.logical_and(used, s == n_up))
    def _():
        scatter_wait()

    @pl.when(jnp.logical_and(used, s >= n_up))
    def _():
        acc = jnp.dot(act_ref[0], w2_ref[0:tf, :], preferred_element_type=F32)
        for f in range(1, n_up):
            acc = acc + jnp.dot(act_ref[f], w2_ref[f * tf:(f + 1) * tf, :], preferred_element_type=F32)
        val = acc * g_ref[...]
        gather_next()
        for t in range(n_down):
            @pl.when(s == n_up + t)
            def _():
                obuf_ref[:, t * tn:(t + 1) * tn] = val

    @pl.when(jnp.logical_and(i == nused, s == 0))
    def _():
        gather_wait(slot)

        def start(r, carry):
            scatter(r).start()
            return carry

        lax.fori_loop(0, rows, start, 0, unroll=8)
        scatter_wait()


def _moe_call(block_e, nused, slot_tok, slot_dst, slot_gate, hp, w1, w3, w2, layer, n_out_rows, rows, tf, tn):
    _, n_e, d, dff = w1.shape
    nb = slot_tok.shape[0] // rows
    n_up, n_down = dff // tf, d // tn
    assert rows % (n_up + n_down) == 0 and rows % n_up == 0 and slot_dst.shape[0] == (nb + 1) * rows

    def blk(i, nu):
        return jnp.minimum(i, nu[0] - 1)

    def up_map(i, s, be, nu):
        return (layer, be[blk(i, nu)], 0, jnp.where(i < nu[0], jnp.minimum(s, n_up - 1), n_up - 1))

    def down_map(i, s, be, nu):
        return (layer, be[blk(i, nu)], 0, jnp.where(i < nu[0], jnp.maximum(s - n_up, 0), n_down - 1))

    smem_tab = lambda f: pl.BlockSpec((None, 1, rows), f, memory_space=pltpu.SMEM)
    return pl.pallas_call(
        _moe_kernel,
        grid_spec=pltpu.PrefetchScalarGridSpec(
            num_scalar_prefetch=2,
            grid=(nb, n_up + n_down),
            in_specs=[smem_tab(lambda i, s, be, nu: (0, 0, 0)),
                      smem_tab(lambda i, s, be, nu: (jnp.minimum(i + 1, nb - 1), 0, 0)),
                      smem_tab(lambda i, s, be, nu: (i, 0, 0)),
                      pl.BlockSpec(memory_space=pl.ANY),
                      pl.BlockSpec((None, None, d, tf), up_map),
                      pl.BlockSpec((None, None, d, tf), up_map),
                      pl.BlockSpec((None, None, dff, tn), down_map),
                      pl.BlockSpec((rows, 1), lambda i, s, be, nu: (blk(i, nu), 0))],
            out_specs=pl.BlockSpec(memory_space=pl.ANY),
            scratch_shapes=[pltpu.VMEM((2, rows, d // 2), jnp.uint32), pltpu.VMEM((rows, d), BF16),
                            pltpu.VMEM((n_up, rows, tf), BF16), pltpu.VMEM((rows, d), F32),
                            pltpu.SemaphoreType.DMA((2,)), pltpu.SemaphoreType.DMA(())]),
        out_shape=SDS((n_out_rows, d), F32),
        compiler_params=_params(("arbitrary", "arbitrary")),
        name="moe_ffn",
    )(block_e, nused, slot_tok.reshape(nb, 1, rows), slot_tok.reshape(nb, 1, rows),
      slot_dst.reshape(nb + 1, 1, rows), hp, w1, w3, w2, slot_gate.reshape(nb * rows, 1))


def _combine_kernel(x_ref, gate_ref, fw_ref, y0_ref, y1_ref, *o_refs, n_lat_blocks):
    v = x_ref[...] + gate_ref[...] * (y0_ref[...] + y1_ref[...])
    if len(o_refs) == 1:
        o_refs[0][...] = v
        return
    v = v * lax.rsqrt(jnp.mean(v * v, axis=-1, keepdims=True) + EPS) * fw_ref[...]

    @pl.when(pl.program_id(0) < n_lat_blocks)
    def _():
        o_refs[0][...] = v

    @pl.when(pl.program_id(0) >= n_lat_blocks)
    def _():
        o_refs[1][...] = v


def _combine_call(x, mod3, final_w, ys, mod_row, tm, t_lat, final):
    t, d = x.shape
    nl = t_lat // tm
    nc = (t - t_lat) // tm
    if final:
        out_specs = [pl.BlockSpec((tm, d), lambda i: (jnp.minimum(i, nl - 1), 0)),
                     pl.BlockSpec((tm, d), lambda i: (jnp.maximum(i - nl, 0), 0))]
        out_shape = [SDS((t_lat, d), F32), SDS((t - t_lat, d), F32)]
    else:
        out_specs = [pl.BlockSpec((tm, d), lambda i: (i, 0))]
        out_shape = [SDS((t, d), F32)]
    assert nl * tm == t_lat and (nl + nc) * tm == t
    outs = pl.pallas_call(
        functools.partial(_combine_kernel, n_lat_blocks=nl),
        grid=(t // tm,),
        in_specs=[pl.BlockSpec((tm, d), lambda i: (i, 0)),
                  pl.BlockSpec((None, 1, d), lambda i: (mod_row(i, tm) * N_MOD + 5, 0, 0)),
                  pl.BlockSpec((1, d), lambda i: (0, 0)),
                  pl.BlockSpec((tm, d), lambda i: (i, 0)),
                  pl.BlockSpec((tm, d), lambda i: (t // tm + i, 0))],
        out_specs=out_specs, out_shape=out_shape,
        compiler_params=_params(("arbitrary",)),
        name="moe_combine",
    )(x, mod3, final_w.reshape(1, d), ys, ys)
    return outs if final else outs[0]


def _dispatch_tables(idx, gate, n_experts, rows):
    t = idx.shape[1]
    n_assign = t * TOP_K
    flat_e = idx.T.reshape(-1)
    flat_g = gate.T.reshape(-1)
    order = jnp.argsort(flat_e, stable=True).astype(jnp.int32)
    counts = jnp.sum((flat_e[:, None] == jnp.arange(n_experts, dtype=jnp.int32)[None, :]).astype(jnp.int32), axis=0)
    padded = (counts + rows - 1) // rows * rows
    pad_end = jnp.cumsum(padded)
    pad_start = pad_end - padded
    start = jnp.cumsum(counts) - counts
    n_blocks = -(-n_assign // rows) + n_experts
    n_slots = n_blocks * rows
    block_e = jnp.minimum(jnp.searchsorted(pad_end, jnp.arange(n_blocks, dtype=jnp.int32) * rows, side='right'),
                          n_experts - 1).astype(jnp.int32)
    nused = (pad_end[-1] // rows).astype(jnp.int32).reshape(1)
    slot = jnp.arange(n_slots, dtype=jnp.int32)
    slot_e = jnp.repeat(block_e, rows)
    within = slot - pad_start[slot_e]
    valid = jnp.logical_and(within < counts[slot_e], slot < pad_end[-1])
    src = order[jnp.clip(start[slot_e] + within, 0, n_assign - 1)]
    slot_tok = jnp.where(valid, src // TOP_K, 0).astype(jnp.int32)
    slot_gate = jnp.where(valid, flat_g[src], 0.0)
    dump = n_assign + slot % rows
    slot_dst = jnp.where(valid, (src % TOP_K) * t + src // TOP_K, dump).astype(jnp.int32)
    slot_dst = jnp.concatenate([n_assign + jnp.arange(rows, dtype=jnp.int32), slot_dst])
    return block_e, nused, slot_tok, slot_dst, slot_gate


def _tile(n, target):
    if n <= target:
        return n
    best = None
    for cand in range(LANES, target + 1, LANES):
        if n % cand == 0:
            best = cand
    assert best is not None, (n, target)
    return best


def kernel(x_prompt, x_sample, state_gla, state_gdn, c, c_ctx, ln1_w, w_mod, b_mod, w_in, gla_w2, gla_b2, gla_norm_w, gdn_conv_w, gdn_a_log, gdn_dt_bias, gdn_norm_w, w_out, ln2_w, router_w, router_bias, w1, w3, w2, final_norm_w):
    bc, lc, d = x_prompt.shape
    bl, ll, _ = x_sample.shape
    depth = w_in.shape[0]
    gh, gdk, gdv = state_gla.shape[3:]
    dh, ddk, ddv = state_gdn.shape[3:]
    rank = gla_w2.shape[2]
    n_experts = router_w.shape[1]
    qk_w, d_gla, d_gdn = gh * gdk, gh * gdv, dh * ddv
    t_lat, t_ctx = bl * ll, bc * lc
    assert ddk == LANES and ddv == LANES and t_lat % t_ctx == 0 and ll % GRID_W == 0

    x = jnp.concatenate([x_sample.reshape(t_lat, d), x_prompt.reshape(t_ctx, d)], axis=0)
    n_cond = -(-(bl + 1) // 8) * 8
    cond = jnp.concatenate([c, c_ctx[None, :], jnp.zeros((n_cond - bl - 1, d), F32)], axis=0)
    mod = _mod_call(cond, w_mod, b_mod)

    def mod_row(i, tm):
        return jnp.minimum(i * tm // ll, bl)

    o_glr = 2 * qk_w + 2 * d_gla
    o_dqkv = o_glr + 2 * rank
    o_da = o_dqkv + 4 * d_gdn
    n_small = 2 * rank + 4 * dh
    small_w = -(-n_small // LANES) * LANES
    gr_col = 2 * qk_w + d_gla
    dqkv_col = gr_col + d_gla
    dz_col = dqkv_col + 3 * d_gdn

    tm = math.gcd(math.gcd(ll, t_ctx), 512)
    w_main = jnp.concatenate([w_in[:, :, :o_glr], w_in[:, :, o_dqkv:o_da]], axis=2).astype(BF16)
    w_small = jnp.concatenate([w_in[:, :, o_glr:o_dqkv], w_in[:, :, o_da:],
                               jnp.zeros((depth, d, small_w - n_small), F32)], axis=2).astype(BF16)
    w_out_b, w1_b, w3_b, w2_b = (w.astype(BF16) for w in (w_out, w1, w3, w2))
    gla_states, gdn_states = [], []
    for l in range(depth):
        mod3 = mod[l].reshape(n_cond * N_MOD, 1, d)
        proj_main, proj_small = _in_proj_call(x, ln1_w[l], mod3, w_main, w_small, l, mod_row, tm,
                                              _tile(w_main.shape[2], 1024))

        og_lat, _ = _gla_call(proj_main, proj_small, gla_w2[l], gla_b2[l], gla_norm_w[l], state_gla, l,
                              0, bl, ll, gh, gdk, gdv, rank)
        og_ctx, sg = _gla_call(proj_main, proj_small, gla_w2[l], gla_b2[l], gla_norm_w[l], None, l,
                               t_lat // lc, bc, lc, gh, gdk, gdv, rank)
        gla_states.append(sg)

        conv_lat = _conv_call(proj_main, gdn_conv_w[l], dqkv_col, 0, bl, ll, GRID_W, True, dh, ddk)
        conv_ctx = _conv_call(proj_main, gdn_conv_w[l], dqkv_col, t_lat // t_ctx, 1, t_ctx, lc, False, dh, ddk)
        hg = math.gcd(dh, 8)
        od_lat, _ = _gdn_call(conv_lat, proj_small, gdn_a_log[l], gdn_dt_bias[l], gdn_norm_w[l], state_gdn, l,
                              0, bl, ll, dh, hg, 2 * rank)
        od_ctx, sd = _gdn_call(conv_ctx, proj_small, gdn_a_log[l], gdn_dt_bias[l], gdn_norm_w[l], None, l,
                               t_lat // lc, bc, lc, dh, hg, 2 * rank)
        gdn_states.append(sd)

        x1 = _out_proj_call(og_lat, og_ctx, od_lat, od_ctx, proj_main, gr_col, dz_col, x, mod3,
                            w_out_b, l, mod_row, tm, _tile(d, 1024))

        hp, idx, gate = _router_call(x1, ln2_w[l], mod3, router_w, router_bias, mod_row, tm)
        block_e, nused, slot_tok, slot_dst, slot_gate = _dispatch_tables(idx, gate, n_experts, MOE_ROWS)
        ys = _moe_call(block_e, nused, slot_tok, slot_dst, slot_gate, hp, w1_b, w3_b, w2_b, l,
                       TOP_K * (t_lat + t_ctx) + MOE_ROWS, MOE_ROWS, _tile(w1.shape[-1], 512), _tile(d, 1024))
        x = _combine_call(x1, mod3, final_norm_w, ys, mod_row, _tile(tm, 256), t_lat, l == depth - 1)

    y_sample = x[0].reshape(bl, ll, d)
    y_prompt = x[1].reshape(bc, lc, d)
    return (y_prompt, y_sample, jnp.stack(gla_states, axis=1), jnp.stack(gdn_states, axis=1))
```

```python
import functools
import math

import jax
import jax.numpy as jnp
from jax import lax
from jax.experimental import pallas as pl
from jax.experimental.pallas import tpu as pltpu

F32 = jnp.float32
BF16 = jnp.bfloat16
SDS = jax.ShapeDtypeStruct

EPS = 1e-6
CHUNK = 64
GRID_W = 64
GLA_TAU = 16.0
N_GROUPS = 4
TOP_K = 2
N_MOD = 6
LANES = 128
MOE_ROWS = 512
VMEM_LIMIT = 56 * 1024 * 1024
VMEM_LIMIT_MAX = 60 * 1024 * 1024

_NT = (((1,), (1,)), ((), ()))
_TN = (((0,), (0,)), ((), ()))


def _params(sem, vmem=VMEM_LIMIT):
    return pltpu.CompilerParams(dimension_semantics=sem, vmem_limit_bytes=vmem)


def _sigmoid(x):
    return 1.0 / (1.0 + jnp.exp(-x))


def _silu(x):
    return x * _sigmoid(x)


def _softplus(x):
    return jnp.maximum(x, 0.0) + jnp.log1p(jnp.exp(-jnp.abs(x)))


def _split_bf16(x):
    hi = x.astype(BF16)
    lo = (x - hi.astype(F32)).astype(BF16)
    return hi, lo


def _tri_dot(tri, x):
    hi, lo = _split_bf16(x)
    return (jnp.dot(tri, hi, preferred_element_type=F32)
            + jnp.dot(tri, lo, preferred_element_type=F32))


def _tri_masks(rev):
    r = lax.broadcasted_iota(jnp.int32, (CHUNK, CHUNK), 0)
    c = lax.broadcasted_iota(jnp.int32, (CHUNK, CHUNK), 1)
    if rev:
        return r <= c, r < c
    return r >= c, r > c


def _mod_kernel(c_ref, w_ref, b_ref, o_ref):
    a = _silu(c_ref[...]).astype(BF16)
    o_ref[...] = jnp.dot(a, w_ref[...].astype(BF16), preferred_element_type=F32) + b_ref[...]


def _mod_call(cond, w_mod, b_mod):
    depth, d, n = w_mod.shape
    r = cond.shape[0]
    tn = 512
    return pl.pallas_call(
        _mod_kernel,
        grid=(depth, n // tn),
        in_specs=[pl.BlockSpec((r, d), lambda l, j: (0, 0)),
                  pl.BlockSpec((None, d, tn), lambda l, j: (l, 0, j)),
                  pl.BlockSpec((None, 1, tn), lambda l, j: (l, 0, j))],
        out_specs=pl.BlockSpec((None, r, tn), lambda l, j: (l, 0, j)),
        out_shape=SDS((depth, r, n), F32),
        compiler_params=_params(("arbitrary", "arbitrary")),
        name="mod",
    )(cond, w_mod, b_mod.reshape(depth, 1, n))


def _in_proj_kernel(x_ref, ln_ref, shift_ref, scale_ref, w_ref, ws_ref, o_ref, os_ref, h_ref):
    @pl.when(pl.program_id(1) == 0)
    def _():
        x = x_ref[...]
        y = x * lax.rsqrt(jnp.mean(x * x, axis=-1, keepdims=True) + EPS) * ln_ref[...]
        h = (y * (1.0 + scale_ref[...]) + shift_ref[...]).astype(BF16)
        h_ref[...] = h
        os_ref[...] = jnp.dot(h, ws_ref[...], preferred_element_type=F32)

    o_ref[...] = jnp.dot(h_ref[...], w_ref[...], preferred_element_type=F32).astype(o_ref.dtype)


def _in_proj_call(x, ln_w, mod3, w_main, w_small, layer, mod_row, tm, tn):
    t, d = x.shape
    nm = w_main.shape[2]
    ns = w_small.shape[2]
    return pl.pallas_call(
        _in_proj_kernel,
        grid=(t // tm, nm // tn),
        in_specs=[pl.BlockSpec((tm, d), lambda i, j: (i, 0)),
                  pl.BlockSpec((1, d), lambda i, j: (0, 0)),
                  pl.BlockSpec((None, 1, d), lambda i, j: (mod_row(i, tm) * N_MOD + 0, 0, 0)),
                  pl.BlockSpec((None, 1, d), lambda i, j: (mod_row(i, tm) * N_MOD + 1, 0, 0)),
                  pl.BlockSpec((None, d, tn), lambda i, j: (layer, 0, j)),
                  pl.BlockSpec((None, d, ns), lambda i, j: (layer, 0, 0))],
        out_specs=[pl.BlockSpec((tm, tn), lambda i, j: (i, j)),
                   pl.BlockSpec((tm, ns), lambda i, j: (i, 0))],
        out_shape=[SDS((t, nm), BF16), SDS((t, ns), F32)],
        scratch_shapes=[pltpu.VMEM((tm, d), BF16)],
        compiler_params=_params(("arbitrary", "arbitrary")),
        name="in_proj",
    )(x, ln_w.reshape(1, d), mod3, mod3, w_main, w_small)


def _gla_iter(i, n, per, first, rank, q_ref, k_ref, v_ref, lr_ref, w2_ref, b2_ref, nw_ref, o_ref, st_ref, oacc_ref):
    dk = q_ref.shape[-1]
    jobs = [(s, (n - 1 - (i * per + u)) if s else (i * per + u)) for s in range(2) for u in range(per)]
    nj = len(jobs)
    rows = [pl.ds(pl.multiple_of(c * CHUNK, CHUNK), CHUNK) for s, c in jobs]
    incl = [_tri_masks(False)[0], _tri_masks(True)[0]]
    tri = [m.astype(BF16) for m in incl]
    w2 = [w2_ref[s].astype(BF16) for s in range(2)]
    q = [q_ref[rows[j], :].astype(F32) * dk ** -0.5 for j in range(nj)]
    k = [k_ref[rows[j], :].astype(F32) for j in range(nj)]
    v = [v_ref[rows[j], :] for j in range(nj)]
    lr = [lr_ref[rows[j], :][:, s * rank:(s + 1) * rank].astype(BF16) for j, (s, c) in enumerate(jobs)]
    pre = [jnp.dot(lr[j], w2[s], preferred_element_type=F32) + b2_ref[s] for j, (s, c) in enumerate(jobs)]
    g = [-_softplus(-x) / GLA_TAU for x in pre]
    cum = [_tri_dot(tri[s], g[j]) for j, (s, c) in enumerate(jobs)]
    cref = [cum[j][(CHUNK - 1 - CHUNK // 2 if s else CHUNK // 2):(CHUNK - CHUNK // 2 if s else CHUNK // 2 + 1), :]
            for j, (s, c) in enumerate(jobs)]
    clast = [cum[j][(0 if s else CHUNK - 1):(1 if s else CHUNK), :] for j, (s, c) in enumerate(jobs)]
    qa = [(q[j] * jnp.exp(cum[j] - cref[j])).astype(BF16) for j in range(nj)]
    ka = [(k[j] * jnp.exp(cref[j] - cum[j])).astype(BF16) for j in range(nj)]
    kd = [(k[j] * jnp.exp(clast[j] - cum[j])).astype(BF16) for j in range(nj)]
    qd = [(q[j] * jnp.exp(cum[j])).astype(BF16) for j in range(nj)]
    att = [lax.dot_general(qa[j], ka[j], _NT, preferred_element_type=F32) for j in range(nj)]
    kv = [lax.dot_general(v[j], kd[j], _TN, preferred_element_type=F32) for j in range(nj)]
    att = [jnp.where(incl[s], att[j], 0.0).astype(BF16) for j, (s, c) in enumerate(jobs)]
    o = [jnp.dot(att[j], v[j], preferred_element_type=F32) for j in range(nj)]
    st = [st_ref[0], st_ref[1]]
    for j, (s, c) in enumerate(jobs):
        o[j] = o[j] + lax.dot_general(qd[j], st[s].astype(BF16), _NT, preferred_element_type=F32)
        st[s] = st[s] * jnp.exp(clast[j]) + kv[j]
    for s in range(2):
        st_ref[s] = st[s]
    for j in range(nj):
        if first:
            oacc_ref[rows[j], :] = o[j]
        else:
            tot = oacc_ref[rows[j], :] + o[j]
            y = tot * lax.rsqrt(jnp.mean(tot * tot, axis=-1, keepdims=True) + EPS) * nw_ref[...]
            o_ref[rows[j], :] = y.astype(o_ref.dtype)


def _gla_kernel(*refs, rank, has_init, want_state):
    q_ref, k_ref, v_ref, lr_ref, w2_ref, b2_ref, nw_ref = refs[:7]
    refs = refs[7:]
    if has_init:
        s0_ref, refs = refs[0], refs[1:]
    o_ref, refs = refs[0], refs[1:]
    if want_state:
        sfin_ref, refs = refs[0], refs[1:]
    st_ref, oacc_ref = refs
    n = q_ref.shape[0] // CHUNK
    for s in range(2):
        if has_init:
            st_ref[s] = s0_ref[s].T
        else:
            st_ref[s] = jnp.zeros(st_ref.shape[1:], F32)
    common = (rank, q_ref, k_ref, v_ref, lr_ref, w2_ref, b2_ref, nw_ref, o_ref, st_ref, oacc_ref)
    per = 4 if n % 8 == 0 else (2 if n % 4 == 0 else 1)

    def half(first):
        def body(i, carry):
            _gla_iter(i, n, per, first, *common)
            return carry
        return body

    lax.fori_loop(0, n // (2 * per), half(True), 0)
    lax.fori_loop(n // (2 * per), n // per, half(False), 0)
    if want_state:
        for s in range(2):
            sfin_ref[s] = st_ref[s].T


def _gla_call(proj_main, proj_small, w2, b2, norm_w, state, layer, row0, batch, seq, heads, dk, dv, rank):
    has_init = state is not None
    want_state = not has_init
    qk_w = heads * dk
    assert seq % (2 * CHUNK) == 0 and (2 * qk_w) % dv == 0
    in_specs = [pl.BlockSpec((seq, dk), lambda b, h: (row0 + b, h)),
                pl.BlockSpec((seq, dk), lambda b, h: (row0 + b, heads + h)),
                pl.BlockSpec((seq, dv), lambda b, h: (row0 + b, 2 * qk_w // dv + h)),
                pl.BlockSpec((seq, proj_small.shape[1]), lambda b, h: (row0 + b, 0)),
                pl.BlockSpec((2, rank, dk), lambda b, h: (0, 0, h)),
                pl.BlockSpec((2, 1, dk), lambda b, h: (0, 0, h)),
                pl.BlockSpec((1, dv), lambda b, h: (0, 0))]
    args = [proj_main, proj_main, proj_main, proj_small, w2, b2.reshape(2, 1, qk_w), norm_w.reshape(1, dv)]
    if has_init:
        in_specs.append(pl.BlockSpec((None, None, 2, None, dk, dv), lambda b, h: (b, layer, 0, h, 0, 0)))
        args.append(state)
    out_specs = [pl.BlockSpec((seq, dv), lambda b, h: (b, h))]
    out_shape = [SDS((batch * seq, heads * dv), BF16)]
    if want_state:
        out_specs.append(pl.BlockSpec((None, 2, None, dk, dv), lambda b, h: (b, 0, h, 0, 0)))
        out_shape.append(SDS((batch, 2, heads, dk, dv), F32))
    outs = pl.pallas_call(
        functools.partial(_gla_kernel, rank=rank, has_init=has_init, want_state=want_state),
        grid=(batch, heads),
        in_specs=in_specs, out_specs=out_specs, out_shape=out_shape,
        scratch_shapes=[pltpu.VMEM((2, dv, dk), F32), pltpu.VMEM((seq, dv), F32)],
        compiler_params=_params(("arbitrary", "arbitrary")),
        name="gla",
    )(*args)
    return outs if want_state else (outs[0], None)


def _conv_kernel(x_ref, w_ref, o_ref, xs_ref, *, width, vertical, n_qk_blocks, n_q_blocks, qscale):
    seq = x_ref.shape[0]
    n_rows = seq // width
    cb = pl.program_id(1)
    w = w_ref[...]
    ridx = lax.broadcasted_iota(jnp.int32, (width, LANES), 0)
    tile_rows = lambda r: pl.ds(pl.multiple_of(r * width, width), width)

    for t in range(3):
        xs_ref[t, tile_rows(0), :] = jnp.zeros((width, LANES), F32)
        xs_ref[t, tile_rows(n_rows + 1), :] = jnp.zeros((width, LANES), F32)

    def shift_body(r, carry):
        tile = x_ref[tile_rows(r), :].astype(F32)
        xs_ref[1, tile_rows(r + 1), :] = tile
        for dc in (-1, 1):
            sh = pltpu.roll(tile, (-dc) % width, 0)
            edge = width - 1 if dc == 1 else 0
            xs_ref[dc + 1, tile_rows(r + 1), :] = jnp.where(ridx == edge, 0.0, sh)
        return carry

    unroll = max(1, min(n_rows, 8 * GRID_W // width))
    lax.fori_loop(0, n_rows, shift_body, 0, unroll=unroll)

    def body(r, carry):
        acc = jnp.zeros((width, LANES), F32)
        for dr in (-1, 0, 1):
            if not vertical and dr != 0:
                continue
            for dc in (-1, 0, 1):
                tap = (dr + 1) * 3 + (dc + 1)
                acc = acc + xs_ref[dc + 1, tile_rows(r + dr + 1), :] * w[tap:tap + 1, :]
        y = _silu(acc)
        yn = y * lax.rsqrt(jnp.sum(y * y, axis=-1, keepdims=True) + EPS)
        yn = yn * jnp.where(cb < n_q_blocks, qscale, 1.0)
        y = jnp.where(cb < n_qk_blocks, yn, y)
        o_ref[tile_rows(r), :] = y.astype(o_ref.dtype)
        return carry

    lax.fori_loop(0, n_rows, body, 0, unroll=unroll)


def _conv_call(proj_main, conv_w, col0, row0, batch, seq, width, vertical, heads, dk):
    ch = conv_w.shape[-1]
    nb = ch // LANES
    assert dk == LANES and col0 % LANES == 0 and seq % width == 0
    return pl.pallas_call(
        functools.partial(_conv_kernel, width=width, vertical=vertical, n_qk_blocks=2 * heads, n_q_blocks=heads,
                          qscale=dk ** -0.5),
        grid=(batch, nb),
        in_specs=[pl.BlockSpec((seq, LANES), lambda b, c: (row0 + b, col0 // LANES + c)),
                  pl.BlockSpec((9, LANES), lambda b, c: (0, c))],
        out_specs=pl.BlockSpec((seq, LANES), lambda b, c: (b, c)),
        out_shape=SDS((batch * seq, ch), BF16),
        scratch_shapes=[pltpu.VMEM((3, seq + 2 * width, LANES), F32)],
        compiler_params=_params(("arbitrary", "arbitrary")),
        name="gdn_conv",
    )(proj_main, conv_w.reshape(9, ch))


def _gdn_iter(cf, cb, first, hg, n_ab, q_ref, k_ref, v_ref, lr_ref, arow_ref, dtrow_ref, nw_ref, o_ref, st_ref,
              oacc_ref, sel_ref):
    d = LANES
    chains = [(s, hh) for s in range(2) for hh in range(hg)]
    rows = [pl.ds(pl.multiple_of(c * CHUNK, CHUNK), CHUNK) for c in (cf, cb)]
    masks = [_tri_masks(False), _tri_masks(True)]
    tri = [m[0].astype(BF16) for m in masks]
    lane = lax.broadcasted_iota(jnp.int32, (CHUNK, d), 1)
    strict_w = [jnp.logical_and(lane < CHUNK, r_c) for r_c in
                (lax.broadcasted_iota(jnp.int32, (CHUNK, d), 0) > lane, lax.broadcasted_iota(jnp.int32, (CHUNK, d), 0) < lane)]

    tiles = []
    for s in range(2):
        lr = lr_ref[rows[s], :]
        g_t = -jnp.exp(arow_ref[...]) * _softplus(lr + dtrow_ref[...])
        tiles.append(_split_bf16(jnp.where(lane < n_ab, g_t, _sigmoid(lr))))
    q = [q_ref[rows[s], hh * d:(hh + 1) * d] for s, hh in chains]
    k = [k_ref[rows[s], hh * d:(hh + 1) * d] for s, hh in chains]
    v = [v_ref[rows[s], hh * d:(hh + 1) * d].astype(F32) for s, hh in chains]
    kf = [x.astype(F32) for x in k]

    sel = [jnp.dot(tiles[s][0], sel_ref[g], preferred_element_type=F32)
           + jnp.dot(tiles[s][1], sel_ref[g], preferred_element_type=F32) for g, (s, hh) in enumerate(chains)]
    qkk = [lax.dot_general(jnp.concatenate([q[g], k[g]], axis=0), k[g], _NT, preferred_element_type=F32)
           for g in range(len(chains))]
    gb = [x[:, :d] for x in sel]
    beta = [x[:, d:] for x in sel]
    tr = [_tri_dot(tri[s], jnp.concatenate([gb[g], jnp.where(strict_w[s], gb[g], 0.0)], axis=1))
          for g, (s, hh) in enumerate(chains)]
    cum = [x[:, :d] for x in tr]
    gamma = [jnp.where(masks[s][0], jnp.exp(jnp.where(masks[s][0], tr[g][:, d:d + CHUNK], 0.0)), 0.0)
             for g, (s, hh) in enumerate(chains)]
    qk = [qkk[g][:CHUNK] * gamma[g] for g in range(len(chains))]
    m = [jnp.where(masks[s][1], beta[g][:, :CHUNK] * qkk[g][CHUNK:] * gamma[g], 0.0)
         for g, (s, hh) in enumerate(chains)]
    y = [-x for x in m]
    mb = [x.astype(BF16) for x in m]
    p = [jnp.dot(x, x, preferred_element_type=F32) for x in mb]
    steps = (CHUNK - 1).bit_length() - 1
    for it in range(steps):
        pb = [x.astype(BF16) for x in p]
        if it + 1 < steps:
            yp = [jnp.dot(jnp.concatenate([y[g].astype(BF16), pb[g]], axis=0), pb[g], preferred_element_type=F32)
                  for g in range(len(chains))]
            y = [y[g] + p[g] + yp[g][:CHUNK] for g in range(len(chains))]
            p = [x[CHUNK:] for x in yp]
        else:
            yp = [jnp.dot(y[g].astype(BF16), pb[g], preferred_element_type=F32) for g in range(len(chains))]
            y = [y[g] + p[g] + yp[g] for g in range(len(chains))]
    rhs = [jnp.concatenate([v[g] * beta[g], kf[g] * (beta[g] * jnp.exp(cum[g]))], axis=1) for g in range(len(chains))]
    sol = [rhs[g] + jnp.dot(y[g].astype(BF16), rhs[g].astype(BF16), preferred_element_type=F32)
           for g in range(len(chains))]
    clast = [cum[g][(0 if s else CHUNK - 1):(1 if s else CHUNK), :] for g, (s, hh) in enumerate(chains)]
    qd = [q[g].astype(F32) * jnp.exp(cum[g]) for g in range(len(chains))]
    kd = [(kf[g] * jnp.exp(clast[g] - cum[g])).astype(BF16) for g in range(len(chains))]
    st = [st_ref[s, hh] for s, hh in chains]
    wq = [jnp.dot(jnp.concatenate([sol[g][:, d:].astype(BF16), qd[g].astype(BF16)], axis=0), st[g].astype(BF16),
                  preferred_element_type=F32) for g in range(len(chains))]
    vb = [(sol[g][:, :d] - wq[g][:CHUNK]).astype(BF16) for g in range(len(chains))]
    o = [wq[g][CHUNK:] + jnp.dot(qk[g].astype(BF16), vb[g], preferred_element_type=F32) for g in range(len(chains))]
    st_new = [st[g] * jnp.exp(clast[g]) + lax.dot_general(kd[g], vb[g], _TN, preferred_element_type=F32)
              for g in range(len(chains))]
    for g, (s, hh) in enumerate(chains):
        st_ref[s, hh] = st_new[g]
        cols = slice(hh * d, (hh + 1) * d)
        if first:
            oacc_ref[rows[s], cols] = o[g]
        else:
            tot = oacc_ref[rows[s], cols] + o[g]
            yn = tot * lax.rsqrt(jnp.mean(tot * tot, axis=-1, keepdims=True) + EPS) * nw_ref[...]
            o_ref[rows[s], cols] = yn.astype(o_ref.dtype)


def _gdn_iter_paired(i, n, per, first, hg, n_ab, q_ref, k_ref, v_ref, lr_ref, arow_ref, dtrow_ref, nw_ref, o_ref,
                     st_ref, oacc_ref, sel_ref):
    d = LANES
    chains = [(s, u, hh) for s in range(2) for u in range(per) for hh in range(hg)]
    pairs = [(s, u, 2 * p) for s in range(2) for u in range(per) for p in range(hg // 2)]
    nc, npair = len(chains), len(pairs)
    rows = {(s, u): pl.ds(pl.multiple_of(((n - 1 - (i * per + u)) if s else (i * per + u)) * CHUNK, CHUNK), CHUNK)
            for s in range(2) for u in range(per)}
    tri = [_tri_masks(rev)[0].astype(BF16) for rev in (False, True)]
    lane = lax.broadcasted_iota(jnp.int32, (CHUNK, d), 1)
    row = lax.broadcasted_iota(jnp.int32, (CHUNK, d), 0)
    left = lane < CHUNK
    col = jnp.where(left, lane, lane - CHUNK)
    incl_p = [row >= col, row <= col]
    strict_p = [row > col, row < col]
    zero_b = jnp.zeros((CHUNK, d), BF16)

    def lanes(a, b):
        return jnp.concatenate([a, b], axis=1)

    def stack(a, b):
        return jnp.concatenate([a, b], axis=0)

    def bdiag(x):
        z = jnp.zeros_like(x)
        return stack(jnp.where(left, x, z), jnp.where(left, z, x))

    tiles = {}
    for su in rows:
        lr = lr_ref[rows[su], :]
        g_t = -jnp.exp(arow_ref[...]) * _softplus(lr + dtrow_ref[...])
        tiles[su] = _split_bf16(jnp.where(lane < n_ab, g_t, _sigmoid(lr)))
    q = [q_ref[rows[s, u], hh * d:(hh + 1) * d] for s, u, hh in chains]
    k = [k_ref[rows[s, u], hh * d:(hh + 1) * d] for s, u, hh in chains]
    v = [v_ref[rows[s, u], hh * d:(hh + 1) * d].astype(F32) for s, u, hh in chains]
    kf = [x.astype(F32) for x in k]

    sel = [jnp.dot(tiles[s, u][0], sel_ref[s * hg + hh], preferred_element_type=F32)
           + jnp.dot(tiles[s, u][1], sel_ref[s * hg + hh], preferred_element_type=F32) for s, u, hh in chains]
    qkk = [lax.dot_general(stack(lanes(q[2 * p], q[2 * p + 1]), lanes(k[2 * p], k[2 * p + 1])),
                           stack(lanes(k[2 * p], zero_b), lanes(zero_b, k[2 * p + 1])), _NT,
                           preferred_element_type=F32) for p in range(npair)]
    gb = [x[:, :d] for x in sel]
    beta = [x[:, d:] for x in sel]
    beta_p = [jnp.where(left, beta[2 * p], beta[2 * p + 1]) for p in range(npair)]
    tr = [_tri_dot(tri[s], jnp.concatenate(
        [gb[2 * p], gb[2 * p + 1], jnp.where(strict_p[s], jnp.where(left, gb[2 * p], gb[2 * p + 1]), 0.0)], axis=1))
        for p, (s, u, hh) in enumerate(pairs)]
    cum = [tr[g // 2][:, (g % 2) * d:(g % 2 + 1) * d] for g in range(nc)]
    gamma = [jnp.where(incl_p[s], jnp.exp(jnp.where(incl_p[s], tr[p][:, 2 * d:], 0.0)), 0.0)
             for p, (s, u, hh) in enumerate(pairs)]
    qk = [qkk[p][:CHUNK] * gamma[p] for p in range(npair)]
    m = [jnp.where(strict_p[s], beta_p[p] * qkk[p][CHUNK:] * gamma[p], 0.0) for p, (s, u, hh) in enumerate(pairs)]
    y = [-x for x in m]
    mb = [x.astype(BF16) for x in m]
    pw = [jnp.dot(x, bdiag(x), preferred_element_type=F32) for x in mb]
    steps = (CHUNK - 1).bit_length() - 1
    for it in range(steps):
        pb = [x.astype(BF16) for x in pw]
        if it + 1 < steps:
            yp = [jnp.dot(stack(y[p].astype(BF16), pb[p]), bdiag(pb[p]), preferred_element_type=F32)
                  for p in range(npair)]
            y = [y[p] + pw[p] + yp[p][:CHUNK] for p in range(npair)]
            pw = [x[CHUNK:] for x in yp]
        else:
            yp = [jnp.dot(y[p].astype(BF16), bdiag(pb[p]), preferred_element_type=F32) for p in range(npair)]
            y = [y[p] + pw[p] + yp[p] for p in range(npair)]
    rhs = [lanes(v[g] * beta[g], kf[g] * (beta[g] * jnp.exp(cum[g]))) for g in range(nc)]
    zero_r = jnp.zeros((CHUNK, 2 * d), BF16)
    ysol = [jnp.dot(y[p].astype(BF16), stack(lanes(rhs[2 * p].astype(BF16), zero_r), lanes(zero_r, rhs[2 * p + 1].astype(BF16))),
                    preferred_element_type=F32) for p in range(npair)]
    sol = [rhs[g] + ysol[g // 2][:, (g % 2) * 2 * d:(g % 2 + 1) * 2 * d] for g in range(nc)]
    clast = [cum[g][(0 if s else CHUNK - 1):(1 if s else CHUNK), :] for g, (s, u, hh) in enumerate(chains)]
    qd = [(q[g].astype(F32) * jnp.exp(cum[g])).astype(BF16) for g in range(nc)]
    kd = [(kf[g] * jnp.exp(clast[g] - cum[g])).astype(BF16) for g in range(nc)]
    wb = [sol[g][:, d:].astype(BF16) for g in range(nc)]
    dec = [jnp.exp(clast[g]) for g in range(nc)]
    st = {(s, hh): st_ref[s, hh] for s in range(2) for hh in range(hg)}
    for u in range(per):
        gs = [g for g, c in enumerate(chains) if c[1] == u]
        ps = [p for p, c in enumerate(pairs) if c[1] == u]
        wq = {g: jnp.dot(stack(wb[g], qd[g]), st[chains[g][0], chains[g][2]].astype(BF16),
                         preferred_element_type=F32) for g in gs}
        vb = {g: (sol[g][:, :d] - wq[g][:CHUNK]).astype(BF16) for g in gs}
        oq = {p: jnp.dot(qk[p].astype(BF16), stack(lanes(vb[2 * p], zero_b), lanes(zero_b, vb[2 * p + 1])),
                         preferred_element_type=F32) for p in ps}
        for g in gs:
            s, _, hh = chains[g]
            st[s, hh] = st[s, hh] * dec[g] + lax.dot_general(kd[g], vb[g], _TN, preferred_element_type=F32)
            o = wq[g][CHUNK:] + oq[g // 2][:, (g % 2) * d:(g % 2 + 1) * d]
            cols = slice(hh * d, (hh + 1) * d)
            if first:
                oacc_ref[rows[s, u], cols] = o
            else:
                tot = oacc_ref[rows[s, u], cols] + o
                yn = tot * lax.rsqrt(jnp.mean(tot * tot, axis=-1, keepdims=True) + EPS) * nw_ref[...]
                o_ref[rows[s, u], cols] = yn.astype(o_ref.dtype)
    for (s, hh), val in st.items():
        st_ref[s, hh] = val


def _gdn_kernel(*refs, hg, per_chunks, a_off, n_heads, has_init, want_state):
    q_ref, k_ref, v_ref, lr_ref, arow_ref, dtrow_ref, nw_ref = refs[:7]
    refs = refs[7:]
    if has_init:
        s0_ref, refs = refs[0], refs[1:]
    o_ref, refs = refs[0], refs[1:]
    if want_state:
        sfin_ref, refs = refs[0], refs[1:]
    st_ref, oacc_ref, sel_ref = refs
    n = q_ref.shape[0] // CHUNK
    if has_init:
        st_ref[...] = s0_ref[...]
    else:
        st_ref[...] = jnp.zeros(st_ref.shape, F32)
    grp = pl.program_id(1)
    r = lax.broadcasted_iota(jnp.int32, sel_ref.shape[1:], 0)
    c = lax.broadcasted_iota(jnp.int32, sel_ref.shape[1:], 1)
    for s in range(2):
        for hh in range(hg):
            head = grp * hg + hh
            col_a = a_off + s * n_heads + head
            col_b = a_off + (2 + s) * n_heads + head
            sel_ref[s * hg + hh] = jnp.where(r == jnp.where(c < LANES, col_a, col_b), 1.0, 0.0).astype(BF16)
    common = (hg, a_off + 2 * n_heads, q_ref, k_ref, v_ref, lr_ref, arow_ref, dtrow_ref, nw_ref, o_ref, st_ref,
              oacc_ref, sel_ref)

    paired = hg % 2 == 0
    per = per_chunks if (paired and n % (2 * per_chunks) == 0) else 1

    def half(first):
        def body(i, carry):
            if paired:
                _gdn_iter_paired(i, n, per, first, *common)
            else:
                _gdn_iter(i, n - 1 - i, first, *common)
            return carry
        return body

    lax.fori_loop(0, n // (2 * per), half(True), 0)
    lax.fori_loop(n // (2 * per), n // per, half(False), 0)
    if want_state:
        sfin_ref[...] = st_ref[...]


def _gdn_call(conv, proj_small, a_log, dt_bias, norm_w, state, layer, row0, batch, seq, heads, hg, a_off,
              per_chunks=1):
    has_init = state is not None
    want_state = not has_init
    d = LANES
    ng = heads // hg
    small_w = proj_small.shape[1]
    assert heads % hg == 0 and seq % (2 * CHUNK) == 0 and small_w == LANES and a_off + 4 * heads <= small_w
    pad = lambda x: jnp.pad(x.reshape(1, 2 * heads), ((0, 0), (a_off, small_w - a_off - 2 * heads)))
    wide = seq * hg * d
    fits = (3 * 2 * 2 + 2 * 2 + 4) * wide <= VMEM_LIMIT * 3 // 4
    one = {} if fits else dict(pipeline_mode=pl.Buffered(1))
    in_specs = [pl.BlockSpec((seq, hg * d), lambda b, g: (b, g), **one),
                pl.BlockSpec((seq, hg * d), lambda b, g: (b, ng + g), **one),
                pl.BlockSpec((seq, hg * d), lambda b, g: (b, 2 * ng + g), **one),
                pl.BlockSpec((seq, small_w), lambda b, g: (row0 + b, 0), **one),
                pl.BlockSpec((1, small_w), lambda b, g: (0, 0)),
                pl.BlockSpec((1, small_w), lambda b, g: (0, 0)),
                pl.BlockSpec((1, d), lambda b, g: (0, 0))]
    args = [conv, conv, conv, proj_small, pad(a_log), pad(dt_bias), norm_w.reshape(1, d)]
    if has_init:
        in_specs.append(pl.BlockSpec((None, None, 2, hg, d, d), lambda b, g: (b, layer, 0, g, 0, 0), **one))
        args.append(state)
    out_specs = [pl.BlockSpec((seq, hg * d), lambda b, g: (b, g), **one)]
    out_shape = [SDS((batch * seq, heads * d), BF16)]
    if want_state:
        out_specs.append(pl.BlockSpec((None, 2, hg, d, d), lambda b, g: (b, 0, g, 0, 0)))
        out_shape.append(SDS((batch, 2, heads, d, d), F32))
    outs = pl.pallas_call(
        functools.partial(_gdn_kernel, hg=hg, per_chunks=per_chunks, a_off=a_off, n_heads=heads, has_init=has_init,
                          want_state=want_state),
        grid=(batch, ng),
        in_specs=in_specs, out_specs=out_specs, out_shape=out_shape,
        scratch_shapes=[pltpu.VMEM((2, hg, d, d), F32), pltpu.VMEM((seq, hg * d), F32),
                        pltpu.VMEM((2 * hg, LANES, 2 * LANES), BF16)],
        compiler_params=_params(("arbitrary", "arbitrary"), VMEM_LIMIT if fits else VMEM_LIMIT_MAX),
        name="gdn",
    )(*args)
    return outs if want_state else (outs[0], None)


def _out_proj_kernel(gl_ref, gc_ref, dl_ref, dc_ref, gr_ref, dz_ref, x_ref, gate_ref, w_ref, o_ref, lhs_ref,
                     *, n_lat_blocks):
    i = pl.program_id(0)
    half = gl_ref.shape[1]

    def fill(g_ref, d_ref):
        lhs_ref[:, :half] = (g_ref[...].astype(F32) * _silu(gr_ref[...].astype(F32))).astype(BF16)
        lhs_ref[:, half:] = (d_ref[...].astype(F32) * _silu(dz_ref[...].astype(F32))).astype(BF16)

    @pl.when(jnp.logical_and(pl.program_id(1) == 0, i < n_lat_blocks))
    def _():
        fill(gl_ref, dl_ref)

    @pl.when(jnp.logical_and(pl.program_id(1) == 0, i >= n_lat_blocks))
    def _():
        fill(gc_ref, dc_ref)

    mix = jnp.dot(lhs_ref[...], w_ref[...], preferred_element_type=F32)
    o_ref[...] = x_ref[...] + gate_ref[...] * mix


def _out_proj_call(og_lat, og_ctx, od_lat, od_ctx, proj_main, gr_col, dz_col, x, mod3, w_out, layer, mod_row, tm, tn):
    t, d = x.shape
    half = og_lat.shape[1]
    nl = og_lat.shape[0] // tm
    nc = og_ctx.shape[0] // tm
    assert gr_col % half == 0 and dz_col % half == 0 and nl * tm == og_lat.shape[0] and nc * tm == og_ctx.shape[0]
    lat_map = lambda i, j: (jnp.minimum(i, nl - 1), 0)
    ctx_map = lambda i, j: (jnp.clip(i - nl, 0, nc - 1), 0)
    return pl.pallas_call(
        functools.partial(_out_proj_kernel, n_lat_blocks=nl),
        grid=(t // tm, d // tn),
        in_specs=[pl.BlockSpec((tm, half), lat_map), pl.BlockSpec((tm, half), ctx_map),
                  pl.BlockSpec((tm, half), lat_map), pl.BlockSpec((tm, half), ctx_map),
                  pl.BlockSpec((tm, half), lambda i, j: (i, gr_col // half)),
                  pl.BlockSpec((tm, half), lambda i, j: (i, dz_col // half)),
                  pl.BlockSpec((tm, tn), lambda i, j: (i, j)),
                  pl.BlockSpec((None, 1, tn), lambda i, j: (mod_row(i, tm) * N_MOD + 2, 0, j)),
                  pl.BlockSpec((None, 2 * half, tn), lambda i, j: (layer, 0, j))],
        out_specs=pl.BlockSpec((tm, tn), lambda i, j: (i, j)),
        out_shape=SDS((t, d), F32),
        scratch_shapes=[pltpu.VMEM((tm, 2 * half), BF16)],
        compiler_params=_params(("arbitrary", "arbitrary")),
        name="out_proj",
    )(og_lat, og_ctx, od_lat, od_ctx, proj_main, proj_main, x, mod3, w_out)


def _pick4(sel, vals):
    return jnp.where(sel == 0, vals[0], jnp.where(sel == 1, vals[1], jnp.where(sel == 2, vals[2], vals[3])))


def _router_kernel(x_ref, ln_ref, shift_ref, scale_ref, rw_ref, rb_ref, hp_ref, idx_ref, gate_ref, *, n_experts):
    x = x_ref[...]
    y = x * lax.rsqrt(jnp.mean(x * x, axis=-1, keepdims=True) + EPS) * ln_ref[...]
    h = y * (1.0 + scale_ref[...]) + shift_ref[...]
    hb = h.astype(BF16)
    half = h.shape[1] // 2
    bits = pltpu.bitcast(hb.astype(F32), jnp.uint32)
    hp_ref[...] = lax.shift_right_logical(bits[:, :half], jnp.uint32(16)) | bits[:, half:]
    h_hi, h_lo = hb, (h - hb.astype(F32)).astype(BF16)
    w_hi, w_lo = _split_bf16(rw_ref[...])
    logits = (lax.dot_general(w_hi, h_hi, _NT, preferred_element_type=F32)
              + lax.dot_general(w_lo, h_hi, _NT, preferred_element_type=F32)
              + lax.dot_general(w_hi, h_lo, _NT, preferred_element_type=F32))
    mx = jnp.max(logits, axis=0, keepdims=True)
    ex = jnp.exp(logits - mx)
    probs = ex / jnp.sum(ex, axis=0, keepdims=True)
    sel = probs + rb_ref[...]
    per = n_experts // N_GROUPS
    assert per == 4 and TOP_K == 2
    rows_s = [sel[e:e + 1, :] for e in range(n_experts)]
    rows_p = [probs[e:e + 1, :] for e in range(n_experts)]
    best = None
    best_score = None
    for g in range(N_GROUPS):
        r = rows_s[g * per:(g + 1) * per]
        score = None
        for a in range(per):
            for b in range(a + 1, per):
                pair = r[a] + r[b]
                score = pair if score is None else jnp.maximum(score, pair)
        if g == 0:
            best, best_score = jnp.zeros_like(score, dtype=jnp.int32), score
        else:
            better = score > best_score
            best = jnp.where(better, g, best)
            best_score = jnp.where(better, score, best_score)
    sg = [_pick4(best, [rows_s[g * per + a] for g in range(N_GROUPS)]) for a in range(per)]
    pg = [_pick4(best, [rows_p[g * per + a] for g in range(N_GROUPS)]) for a in range(per)]
    i1 = jnp.zeros_like(best)
    m1 = sg[0]
    for a in range(1, per):
        better = sg[a] > m1
        i1 = jnp.where(better, a, i1)
        m1 = jnp.where(better, sg[a], m1)
    i2 = None
    m2 = None
    for a in range(per):
        cand = jnp.where(i1 == a, -jnp.inf, sg[a])
        if i2 is None:
            i2, m2 = jnp.zeros_like(best), cand
        else:
            better = cand > m2
            i2 = jnp.where(better, a, i2)
            m2 = jnp.where(better, cand, m2)
    p1 = _pick4(i1, pg)
    p2 = _pick4(i2, pg)
    tot = p1 + p2
    idx_ref[0:1, :] = best * per + i1
    idx_ref[1:2, :] = best * per + i2
    gate_ref[0:1, :] = p1 / tot
    gate_ref[1:2, :] = p2 / tot


def _router_call(x, ln_w, mod3, router_w, router_bias, mod_row, tm):
    t, d = x.shape
    e = router_w.shape[1]
    return pl.pallas_call(
        functools.partial(_router_kernel, n_experts=e),
        grid=(t // tm,),
        in_specs=[pl.BlockSpec((tm, d), lambda i: (i, 0)),
                  pl.BlockSpec((1, d), lambda i: (0, 0)),
                  pl.BlockSpec((None, 1, d), lambda i: (mod_row(i, tm) * N_MOD + 3, 0, 0)),
                  pl.BlockSpec((None, 1, d), lambda i: (mod_row(i, tm) * N_MOD + 4, 0, 0)),
                  pl.BlockSpec((e, d), lambda i: (0, 0)),
                  pl.BlockSpec((e, 1), lambda i: (0, 0))],
        out_specs=[pl.BlockSpec((tm, d // 2), lambda i: (i, 0)),
                   pl.BlockSpec((TOP_K, tm), lambda i: (0, i)),
                   pl.BlockSpec((TOP_K, tm), lambda i: (0, i))],
        out_shape=[SDS((t, d // 2), jnp.uint32), SDS((TOP_K, t), jnp.int32), SDS((TOP_K, t), F32)],
        compiler_params=_params(("arbitrary",)),
        name="router",
    )(x, ln_w.reshape(1, d), mod3, mod3, router_w.T, router_bias.reshape(e, 1))


def _moe_kernel(be_ref, nused_ref, tok0_ref, tokn_ref, dst_ref, hp_ref, w1_ref, w3_ref, w2_ref, g_ref, y_ref,
                xbuf_ref, xb_ref, act_ref, obuf_ref, gsem, ssem):
    i = pl.program_id(0)
    s = pl.program_id(1)
    n_steps = pl.num_programs(1)
    n_up, rows, tf = act_ref.shape
    tn = w2_ref.shape[1]
    n_down = obuf_ref.shape[1] // tn
    nused = nused_ref[0]
    used = i < nused
    slot = lax.rem(i, 2)
    g_per = rows // (n_up + n_down)
    s_per = rows // n_up

    def gather(tab_ref, r, sl):
        return pltpu.make_async_copy(hp_ref.at[pl.ds(tab_ref[0, r], 1), :], xbuf_ref.at[sl, pl.ds(r, 1), :], gsem.at[sl])

    def scatter(r):
        return pltpu.make_async_copy(obuf_ref.at[pl.ds(r, 1), :], y_ref.at[pl.ds(dst_ref[0, r], 1), :], ssem)

    def gather_wait(sl):
        pltpu.make_async_copy(hp_ref.at[pl.ds(0, rows), :], xbuf_ref.at[sl], gsem.at[sl]).wait()

    def scatter_wait():
        pltpu.make_async_copy(obuf_ref, y_ref.at[pl.ds(0, rows), :], ssem).wait()

    def gather_next():
        for r in range(g_per):
            gather(tokn_ref, s * g_per + r, 1 - slot).start()

    @pl.when(jnp.logical_and(i == 0, s == 0))
    def _():
        obuf_ref[...] = jnp.zeros(obuf_ref.shape, F32)

        def start(r, carry):
            gather(tok0_ref, r, 0).start()
            return carry

        lax.fori_loop(0, rows, start, 0, unroll=8)

    @pl.when(jnp.logical_and(used, s == 0))
    def _():
        gather_wait(slot)
        words = xbuf_ref[slot]
        half = words.shape[1]
        lo = pltpu.bitcast(lax.shift_left(words, jnp.uint32(16)), F32)
        hi = pltpu.bitcast(words & jnp.uint32(0xFFFF0000), F32)
        xb_ref[:, :half] = lo.astype(BF16)
        xb_ref[:, half:] = hi.astype(BF16)

    @pl.when(jnp.logical_and(used, s < n_up))
    def _():
        xb = xb_ref[...]
        a = jnp.dot(xb, w1_ref[...], preferred_element_type=F32)
        b = jnp.dot(xb, w3_ref[...], preferred_element_type=F32)
        act_ref[s] = (_silu(a) * b).astype(BF16)
        gather_next()
        for r in range(s_per):
            scatter(s * s_per + r).start()

    @pl.when(jnp.logical_and(used, s == n_up))
    def _():
        scatter_wait()

    @pl.when(jnp.logical_and(used, s >= n_up))
    def _():
        acc = jnp.dot(act_ref[0], w2_ref[0:tf, :], preferred_element_type=F32)
        for f in range(1, n_up):
            acc = acc + jnp.dot(act_ref[f], w2_ref[f * tf:(f + 1) * tf, :], preferred_element_type=F32)
        val = acc * g_ref[...]
        gather_next()
        for t in range(n_down):
            @pl.when(s == n_up + t)
            def _():
                obuf_ref[:, t * tn:(t + 1) * tn] = val

    @pl.when(jnp.logical_and(i == nused, s == 0))
    def _():
        gather_wait(slot)

        def start(r, carry):
            scatter(r).start()
            return carry

        lax.fori_loop(0, rows, start, 0, unroll=8)
        scatter_wait()


def _moe_call(block_e, nused, slot_tok, slot_dst, slot_gate, hp, w1, w3, w2, layer, n_out_rows, rows, tf, tn):
    _, n_e, d, dff = w1.shape
    nb = slot_tok.shape[0] // rows
    n_up, n_down = dff // tf, d // tn
    assert rows % (n_up + n_down) == 0 and rows % n_up == 0 and slot_dst.shape[0] == (nb + 1) * rows

    def blk(i, nu):
        return jnp.minimum(i, nu[0] - 1)

    def up_map(i, s, be, nu):
        return (layer, be[blk(i, nu)], 0, jnp.where(i < nu[0], jnp.minimum(s, n_up - 1), n_up - 1))

    def down_map(i, s, be, nu):
        return (layer, be[blk(i, nu)], 0, jnp.where(i < nu[0], jnp.maximum(s - n_up, 0), n_down - 1))

    smem_tab = lambda f: pl.BlockSpec((None, 1, rows), f, memory_space=pltpu.SMEM)
    return pl.pallas_call(
        _moe_kernel,
        grid_spec=pltpu.PrefetchScalarGridSpec(
            num_scalar_prefetch=2,
            grid=(nb, n_up + n_down),
            in_specs=[smem_tab(lambda i, s, be, nu: (0, 0, 0)),
                      smem_tab(lambda i, s, be, nu: (jnp.minimum(i + 1, nb - 1), 0, 0)),
                      smem_tab(lambda i, s, be, nu: (i, 0, 0)),
                      pl.BlockSpec(memory_space=pl.ANY),
                      pl.BlockSpec((None, None, d, tf), up_map),
                      pl.BlockSpec((None, None, d, tf), up_map),
                      pl.BlockSpec((None, None, dff, tn), down_map),
                      pl.BlockSpec((rows, 1), lambda i, s, be, nu: (blk(i, nu), 0))],
            out_specs=pl.BlockSpec(memory_space=pl.ANY),
            scratch_shapes=[pltpu.VMEM((2, rows, d // 2), jnp.uint32), pltpu.VMEM((rows, d), BF16),
                            pltpu.VMEM((n_up, rows, tf), BF16), pltpu.VMEM((rows, d), F32),
                            pltpu.SemaphoreType.DMA((2,)), pltpu.SemaphoreType.DMA(())]),
        out_shape=SDS((n_out_rows, d), F32),
        compiler_params=_params(("arbitrary", "arbitrary")),
        name="moe_ffn",
    )(block_e, nused, slot_tok.reshape(nb, 1, rows), slot_tok.reshape(nb, 1, rows),
      slot_dst.reshape(nb + 1, 1, rows), hp, w1, w3, w2, slot_gate.reshape(nb * rows, 1))


def _combine_kernel(x_ref, gate_ref, fw_ref, y0_ref, y1_ref, *o_refs, n_lat_blocks):
    v = x_ref[...] + gate_ref[...] * (y0_ref[...] + y1_ref[...])
    if len(o_refs) == 1:
        o_refs[0][...] = v
        return
    v = v * lax.rsqrt(jnp.mean(v * v, axis=-1, keepdims=True) + EPS) * fw_ref[...]

    @pl.when(pl.program_id(0) < n_lat_blocks)
    def _():
        o_refs[0][...] = v

    @pl.when(pl.program_id(0) >= n_lat_blocks)
    def _():
        o_refs[1][...] = v


def _combine_call(x, mod3, final_w, ys, mod_row, tm, t_lat, final):
    t, d = x.shape
    nl = t_lat // tm
    nc = (t - t_lat) // tm
    if final:
        out_specs = [pl.BlockSpec((tm, d), lambda i: (jnp.minimum(i, nl - 1), 0)),
                     pl.BlockSpec((tm, d), lambda i: (jnp.maximum(i - nl, 0), 0))]
        out_shape = [SDS((t_lat, d), F32), SDS((t - t_lat, d), F32)]
    else:
        out_specs = [pl.BlockSpec((tm, d), lambda i: (i, 0))]
        out_shape = [SDS((t, d), F32)]
    assert nl * tm == t_lat and (nl + nc) * tm == t
    outs = pl.pallas_call(
        functools.partial(_combine_kernel, n_lat_blocks=nl),
        grid=(t // tm,),
        in_specs=[pl.BlockSpec((tm, d), lambda i: (i, 0)),
                  pl.BlockSpec((None, 1, d), lambda i: (mod_row(i, tm) * N_MOD + 5, 0, 0)),
                  pl.BlockSpec((1, d), lambda i: (0, 0)),
                  pl.BlockSpec((tm, d), lambda i: (i, 0)),
                  pl.BlockSpec((tm, d), lambda i: (t // tm + i, 0))],
        out_specs=out_specs, out_shape=out_shape,
        compiler_params=_params(("arbitrary",)),
        name="moe_combine",
    )(x, mod3, final_w.reshape(1, d), ys, ys)
    return outs if final else outs[0]


def _dispatch_tables(idx, gate, n_experts, rows):
    t = idx.shape[1]
    n_assign = t * TOP_K
    flat_e = idx.T.reshape(-1)
    flat_g = gate.T.reshape(-1)
    order = jnp.argsort(flat_e, stable=True).astype(jnp.int32)
    counts = jnp.sum((flat_e[:, None] == jnp.arange(n_experts, dtype=jnp.int32)[None, :]).astype(jnp.int32), axis=0)
    padded = (counts + rows - 1) // rows * rows
    pad_end = jnp.cumsum(padded)
    pad_start = pad_end - padded
    start = jnp.cumsum(counts) - counts
    n_blocks = -(-n_assign // rows) + n_experts
    n_slots = n_blocks * rows
    block_e = jnp.minimum(jnp.searchsorted(pad_end, jnp.arange(n_blocks, dtype=jnp.int32) * rows, side='right'),
                          n_experts - 1).astype(jnp.int32)
    nused = (pad_end[-1] // rows).astype(jnp.int32).reshape(1)
    slot = jnp.arange(n_slots, dtype=jnp.int32)
    slot_e = jnp.repeat(block_e, rows)
    within = slot - pad_start[slot_e]
    valid = jnp.logical_and(within < counts[slot_e], slot < pad_end[-1])
    src = order[jnp.clip(start[slot_e] + within, 0, n_assign - 1)]
    slot_tok = jnp.where(valid, src // TOP_K, 0).astype(jnp.int32)
    slot_gate = jnp.where(valid, flat_g[src], 0.0)
    dump = n_assign + slot % rows
    slot_dst = jnp.where(valid, (src % TOP_K) * t + src // TOP_K, dump).astype(jnp.int32)
    slot_dst = jnp.concatenate([n_assign + jnp.arange(rows, dtype=jnp.int32), slot_dst])
    return block_e, nused, slot_tok, slot_dst, slot_gate


def _tile(n, target):
    if n <= target:
        return n
    best = None
    for cand in range(LANES, target + 1, LANES):
        if n % cand == 0:
            best = cand
    assert best is not None, (n, target)
    return best


def kernel(x_prompt, x_sample, state_gla, state_gdn, c, c_ctx, ln1_w, w_mod, b_mod, w_in, gla_w2, gla_b2, gla_norm_w, gdn_conv_w, gdn_a_log, gdn_dt_bias, gdn_norm_w, w_out, ln2_w, router_w, router_bias, w1, w3, w2, final_norm_w):
    bc, lc, d = x_prompt.shape
    bl, ll, _ = x_sample.shape
    depth = w_in.shape[0]
    gh, gdk, gdv = state_gla.shape[3:]
    dh, ddk, ddv = state_gdn.shape[3:]
    rank = gla_w2.shape[2]
    n_experts = router_w.shape[1]
    qk_w, d_gla, d_gdn = gh * gdk, gh * gdv, dh * ddv
    t_lat, t_ctx = bl * ll, bc * lc
    assert ddk == LANES and ddv == LANES and t_lat % t_ctx == 0 and ll % GRID_W == 0

    x = jnp.concatenate([x_sample.reshape(t_lat, d), x_prompt.reshape(t_ctx, d)], axis=0)
    n_cond = -(-(bl + 1) // 8) * 8
    cond = jnp.concatenate([c, c_ctx[None, :], jnp.zeros((n_cond - bl - 1, d), F32)], axis=0)
    mod = _mod_call(cond, w_mod, b_mod)

    def mod_row(i, tm):
        return jnp.minimum(i * tm // ll, bl)

    o_glr = 2 * qk_w + 2 * d_gla
    o_dqkv = o_glr + 2 * rank
    o_da = o_dqkv + 4 * d_gdn
    n_small = 2 * rank + 4 * dh
    small_w = -(-n_small // LANES) * LANES
    gr_col = 2 * qk_w + d_gla
    dqkv_col = gr_col + d_gla
    dz_col = dqkv_col + 3 * d_gdn

    tm = math.gcd(math.gcd(ll, t_ctx), 512)
    w_main = jnp.concatenate([w_in[:, :, :o_glr], w_in[:, :, o_dqkv:o_da]], axis=2).astype(BF16)
    w_small = jnp.concatenate([w_in[:, :, o_glr:o_dqkv], w_in[:, :, o_da:],
                               jnp.zeros((depth, d, small_w - n_small), F32)], axis=2).astype(BF16)
    w_out_b, w1_b, w3_b, w2_b = (w.astype(BF16) for w in (w_out, w1, w3, w2))
    gla_states, gdn_states = [], []
    for l in range(depth):
        mod3 = mod[l].reshape(n_cond * N_MOD, 1, d)
        proj_main, proj_small = _in_proj_call(x, ln1_w[l], mod3, w_main, w_small, l, mod_row, tm,
                                              _tile(w_main.shape[2], 1024))

        og_lat, _ = _gla_call(proj_main, proj_small, gla_w2[l], gla_b2[l], gla_norm_w[l], state_gla, l,
                              0, bl, ll, gh, gdk, gdv, rank)
        og_ctx, sg = _gla_call(proj_main, proj_small, gla_w2[l], gla_b2[l], gla_norm_w[l], None, l,
                               t_lat // lc, bc, lc, gh, gdk, gdv, rank)
        gla_states.append(sg)

        conv_lat = _conv_call(proj_main, gdn_conv_w[l], dqkv_col, 0, bl, ll, GRID_W, True, dh, ddk)
        conv_ctx = _conv_call(proj_main, gdn_conv_w[l], dqkv_col, t_lat // t_ctx, 1, t_ctx, lc, False, dh, ddk)
        hg = math.gcd(dh, 8)
        od_lat, _ = _gdn_call(conv_lat, proj_small, gdn_a_log[l], gdn_dt_bias[l], gdn_norm_w[l], state_gdn, l,
                              0, bl, ll, dh, hg, 2 * rank, per_chunks=2)
        od_ctx, sd = _gdn_call(conv_ctx, proj_small, gdn_a_log[l], gdn_dt_bias[l], gdn_norm_w[l], None, l,
                               t_lat // lc, bc, lc, dh, hg, 2 * rank, per_chunks=2)
        gdn_states.append(sd)

        x1 = _out_proj_call(og_lat, og_ctx, od_lat, od_ctx, proj_main, gr_col, dz_col, x, mod3,
                            w_out_b, l, mod_row, tm, _tile(d, 1024))

        hp, idx, gate = _router_call(x1, ln2_w[l], mod3, router_w, router_bias, mod_row, tm)
        block_e, nused, slot_tok, slot_dst, slot_gate = _dispatch_tables(idx, gate, n_experts, MOE_ROWS)
        ys = _moe_call(block_e, nused, slot_tok, slot_dst, slot_gate, hp, w1_b, w3_b, w2_b, l,
                       TOP_K * (t_lat + t_ctx) + MOE_ROWS, MOE_ROWS, _tile(w1.shape[-1], 512), _tile(d, 1024))
        x = _combine_call(x1, mod3, final_norm_w, ys, mod_row, _tile(tm, 256), t_lat, l == depth - 1)

    y_sample = x[0].reshape(bl, ll, d)
    y_prompt = x[1].reshape(bc, lc, d)
    return (y_prompt, y_sample, jnp.stack(gla_states, axis=1), jnp.stack(gdn_states, axis=1))
```

```python
import functools
import math

import jax
import jax.numpy as jnp
from jax import lax
from jax.experimental import pallas as pl
from jax.experimental.pallas import tpu as pltpu

F32 = jnp.float32
BF16 = jnp.bfloat16
SDS = jax.ShapeDtypeStruct

EPS = 1e-6
CHUNK = 64
GRID_W = 64
GLA_TAU = 16.0
N_GROUPS = 4
TOP_K = 2
N_MOD = 6
LANES = 128
MOE_ROWS = 512
VMEM_LIMIT = 56 * 1024 * 1024
VMEM_LIMIT_MAX = 60 * 1024 * 1024

_NT = (((1,), (1,)), ((), ()))
_TN = (((0,), (0,)), ((), ()))


def _params(sem, vmem=VMEM_LIMIT):
    return pltpu.CompilerParams(dimension_semantics=sem, vmem_limit_bytes=vmem)


def _sigmoid(x):
    return 1.0 / (1.0 + jnp.exp(-x))


def _silu(x):
    return x * _sigmoid(x)


def _softplus(x):
    return jnp.maximum(x, 0.0) + jnp.log1p(jnp.exp(-jnp.abs(x)))


def _split_bf16(x):
    hi = x.astype(BF16)
    lo = (x - hi.astype(F32)).astype(BF16)
    return hi, lo


def _tri_dot(tri, x):
    hi, lo = _split_bf16(x)
    return (jnp.dot(tri, hi, preferred_element_type=F32)
            + jnp.dot(tri, lo, preferred_element_type=F32))


def _tri_masks(rev):
    r = lax.broadcasted_iota(jnp.int32, (CHUNK, CHUNK), 0)
    c = lax.broadcasted_iota(jnp.int32, (CHUNK, CHUNK), 1)
    if rev:
        return r <= c, r < c
    return r >= c, r > c


def _mod_kernel(c_ref, w_ref, b_ref, o_ref):
    a = _silu(c_ref[...]).astype(BF16)
    o_ref[...] = jnp.dot(a, w_ref[...].astype(BF16), preferred_element_type=F32) + b_ref[...]


def _mod_call(cond, w_mod, b_mod):
    depth, d, n = w_mod.shape
    r = cond.shape[0]
    tn = 512
    return pl.pallas_call(
        _mod_kernel,
        grid=(depth, n // tn),
        in_specs=[pl.BlockSpec((r, d), lambda l, j: (0, 0)),
                  pl.BlockSpec((None, d, tn), lambda l, j: (l, 0, j)),
                  pl.BlockSpec((None, 1, tn), lambda l, j: (l, 0, j))],
        out_specs=pl.BlockSpec((None, r, tn), lambda l, j: (l, 0, j)),
        out_shape=SDS((depth, r, n), F32),
        compiler_params=_params(("arbitrary", "arbitrary")),
        name="mod",
    )(cond, w_mod, b_mod.reshape(depth, 1, n))


def _in_proj_kernel(x_ref, ln_ref, shift_ref, scale_ref, w_ref, ws_ref, o_ref, os_ref, h_ref):
    @pl.when(pl.program_id(1) == 0)
    def _():
        x = x_ref[...]
        y = x * lax.rsqrt(jnp.mean(x * x, axis=-1, keepdims=True) + EPS) * ln_ref[...]
        h = (y * (1.0 + scale_ref[...]) + shift_ref[...]).astype(BF16)
        h_ref[...] = h
        os_ref[...] = jnp.dot(h, ws_ref[...], preferred_element_type=F32)

    o_ref[...] = jnp.dot(h_ref[...], w_ref[...], preferred_element_type=F32).astype(o_ref.dtype)


def _in_proj_call(x, ln_w, mod3, w_main, w_small, layer, mod_row, tm, tn):
    t, d = x.shape
    nm = w_main.shape[2]
    ns = w_small.shape[2]
    return pl.pallas_call(
        _in_proj_kernel,
        grid=(t // tm, nm // tn),
        in_specs=[pl.BlockSpec((tm, d), lambda i, j: (i, 0)),
                  pl.BlockSpec((1, d), lambda i, j: (0, 0)),
                  pl.BlockSpec((None, 1, d), lambda i, j: (mod_row(i, tm) * N_MOD + 0, 0, 0)),
                  pl.BlockSpec((None, 1, d), lambda i, j: (mod_row(i, tm) * N_MOD + 1, 0, 0)),
                  pl.BlockSpec((None, d, tn), lambda i, j: (layer, 0, j)),
                  pl.BlockSpec((None, d, ns), lambda i, j: (layer, 0, 0))],
        out_specs=[pl.BlockSpec((tm, tn), lambda i, j: (i, j)),
                   pl.BlockSpec((tm, ns), lambda i, j: (i, 0))],
        out_shape=[SDS((t, nm), BF16), SDS((t, ns), F32)],
        scratch_shapes=[pltpu.VMEM((tm, d), BF16)],
        compiler_params=_params(("arbitrary", "arbitrary")),
        name="in_proj",
    )(x, ln_w.reshape(1, d), mod3, mod3, w_main, w_small)


def _gla_iter(i, n, per, first, rank, q_ref, k_ref, v_ref, lr_ref, w2_ref, b2_ref, nw_ref, o_ref, st_ref, oacc_ref):
    dk = q_ref.shape[-1]
    jobs = [(s, (n - 1 - (i * per + u)) if s else (i * per + u)) for s in range(2) for u in range(per)]
    nj = len(jobs)
    rows = [pl.ds(pl.multiple_of(c * CHUNK, CHUNK), CHUNK) for s, c in jobs]
    incl = [_tri_masks(False)[0], _tri_masks(True)[0]]
    tri = [m.astype(BF16) for m in incl]
    w2 = [w2_ref[s].astype(BF16) for s in range(2)]
    q = [q_ref[rows[j], :].astype(F32) * dk ** -0.5 for j in range(nj)]
    k = [k_ref[rows[j], :].astype(F32) for j in range(nj)]
    v = [v_ref[rows[j], :] for j in range(nj)]
    lr = [lr_ref[rows[j], :][:, s * rank:(s + 1) * rank].astype(BF16) for j, (s, c) in enumerate(jobs)]
    pre = [jnp.dot(lr[j], w2[s], preferred_element_type=F32) + b2_ref[s] for j, (s, c) in enumerate(jobs)]
    g = [-_softplus(-x) / GLA_TAU for x in pre]
    cum = [_tri_dot(tri[s], g[j]) for j, (s, c) in enumerate(jobs)]
    cref = [cum[j][(CHUNK - 1 - CHUNK // 2 if s else CHUNK // 2):(CHUNK - CHUNK // 2 if s else CHUNK // 2 + 1), :]
            for j, (s, c) in enumerate(jobs)]
    clast = [cum[j][(0 if s else CHUNK - 1):(1 if s else CHUNK), :] for j, (s, c) in enumerate(jobs)]
    qa = [(q[j] * jnp.exp(cum[j] - cref[j])).astype(BF16) for j in range(nj)]
    ka = [(k[j] * jnp.exp(cref[j] - cum[j])).astype(BF16) for j in range(nj)]
    kd = [(k[j] * jnp.exp(clast[j] - cum[j])).astype(BF16) for j in range(nj)]
    qd = [(q[j] * jnp.exp(cum[j])).astype(BF16) for j in range(nj)]
    att = [lax.dot_general(qa[j], ka[j], _NT, preferred_element_type=F32) for j in range(nj)]
    kv = [lax.dot_general(v[j], kd[j], _TN, preferred_element_type=F32) for j in range(nj)]
    att = [jnp.where(incl[s], att[j], 0.0).astype(BF16) for j, (s, c) in enumerate(jobs)]
    o = [jnp.dot(att[j], v[j], preferred_element_type=F32) for j in range(nj)]
    st = [st_ref[0], st_ref[1]]
    for j, (s, c) in enumerate(jobs):
        o[j] = o[j] + lax.dot_general(qd[j], st[s].astype(BF16), _NT, preferred_element_type=F32)
        st[s] = st[s] * jnp.exp(clast[j]) + kv[j]
    for s in range(2):
        st_ref[s] = st[s]
    for j in range(nj):
        if first:
            oacc_ref[rows[j], :] = o[j]
        else:
            tot = oacc_ref[rows[j], :] + o[j]
            y = tot * lax.rsqrt(jnp.mean(tot * tot, axis=-1, keepdims=True) + EPS) * nw_ref[...]
            o_ref[rows[j], :] = y.astype(o_ref.dtype)


def _gla_kernel(*refs, rank, has_init, want_state):
    q_ref, k_ref, v_ref, lr_ref, w2_ref, b2_ref, nw_ref = refs[:7]
    refs = refs[7:]
    if has_init:
        s0_ref, refs = refs[0], refs[1:]
    o_ref, refs = refs[0], refs[1:]
    if want_state:
        sfin_ref, refs = refs[0], refs[1:]
    st_ref, oacc_ref = refs
    n = q_ref.shape[0] // CHUNK
    for s in range(2):
        if has_init:
            st_ref[s] = s0_ref[s].T
        else:
            st_ref[s] = jnp.zeros(st_ref.shape[1:], F32)
    common = (rank, q_ref, k_ref, v_ref, lr_ref, w2_ref, b2_ref, nw_ref, o_ref, st_ref, oacc_ref)
    per = 4 if n % 8 == 0 else (2 if n % 4 == 0 else 1)

    def half(first):
        def body(i, carry):
            _gla_iter(i, n, per, first, *common)
            return carry
        return body

    lax.fori_loop(0, n // (2 * per), half(True), 0)
    lax.fori_loop(n // (2 * per), n // per, half(False), 0)
    if want_state:
        for s in range(2):
            sfin_ref[s] = st_ref[s].T


def _gla_call(proj_main, proj_small, w2, b2, norm_w, state, layer, row0, batch, seq, heads, dk, dv, rank):
    has_init = state is not None
    want_state = not has_init
    qk_w = heads * dk
    assert seq % (2 * CHUNK) == 0 and (2 * qk_w) % dv == 0
    in_specs = [pl.BlockSpec((seq, dk), lambda b, h: (row0 + b, h)),
                pl.BlockSpec((seq, dk), lambda b, h: (row0 + b, heads + h)),
                pl.BlockSpec((seq, dv), lambda b, h: (row0 + b, 2 * qk_w // dv + h)),
                pl.BlockSpec((seq, proj_small.shape[1]), lambda b, h: (row0 + b, 0)),
                pl.BlockSpec((2, rank, dk), lambda b, h: (0, 0, h)),
                pl.BlockSpec((2, 1, dk), lambda b, h: (0, 0, h)),
                pl.BlockSpec((1, dv), lambda b, h: (0, 0))]
    args = [proj_main, proj_main, proj_main, proj_small, w2, b2.reshape(2, 1, qk_w), norm_w.reshape(1, dv)]
    if has_init:
        in_specs.append(pl.BlockSpec((None, None, 2, None, dk, dv), lambda b, h: (b, layer, 0, h, 0, 0)))
        args.append(state)
    out_specs = [pl.BlockSpec((seq, dv), lambda b, h: (b, h))]
    out_shape = [SDS((batch * seq, heads * dv), BF16)]
    if want_state:
        out_specs.append(pl.BlockSpec((None, 2, None, dk, dv), lambda b, h: (b, 0, h, 0, 0)))
        out_shape.append(SDS((batch, 2, heads, dk, dv), F32))
    outs = pl.pallas_call(
        functools.partial(_gla_kernel, rank=rank, has_init=has_init, want_state=want_state),
        grid=(batch, heads),
        in_specs=in_specs, out_specs=out_specs, out_shape=out_shape,
        scratch_shapes=[pltpu.VMEM((2, dv, dk), F32), pltpu.VMEM((seq, dv), F32)],
        compiler_params=_params(("arbitrary", "arbitrary")),
        name="gla",
    )(*args)
    return outs if want_state else (outs[0], None)


def _conv_kernel(x_ref, w_ref, o_ref, xs_ref, *, width, vertical, n_qk_blocks, n_q_blocks, qscale):
    seq = x_ref.shape[0]
    n_rows = seq // width
    cb = pl.program_id(1)
    w = w_ref[...]
    ridx = lax.broadcasted_iota(jnp.int32, (width, LANES), 0)
    tile_rows = lambda r: pl.ds(pl.multiple_of(r * width, width), width)

    for t in range(3):
        xs_ref[t, tile_rows(0), :] = jnp.zeros((width, LANES), F32)
        xs_ref[t, tile_rows(n_rows + 1), :] = jnp.zeros((width, LANES), F32)

    def shift_body(r, carry):
        tile = x_ref[tile_rows(r), :].astype(F32)
        xs_ref[1, tile_rows(r + 1), :] = tile
        for dc in (-1, 1):
            sh = pltpu.roll(tile, (-dc) % width, 0)
            edge = width - 1 if dc == 1 else 0
            xs_ref[dc + 1, tile_rows(r + 1), :] = jnp.where(ridx == edge, 0.0, sh)
        return carry

    unroll = max(1, min(n_rows, 8 * GRID_W // width))
    lax.fori_loop(0, n_rows, shift_body, 0, unroll=unroll)

    def body(r, carry):
        acc = jnp.zeros((width, LANES), F32)
        for dr in (-1, 0, 1):
            if not vertical and dr != 0:
                continue
            for dc in (-1, 0, 1):
                tap = (dr + 1) * 3 + (dc + 1)
                acc = acc + xs_ref[dc + 1, tile_rows(r + dr + 1), :] * w[tap:tap + 1, :]
        y = _silu(acc)
        yn = y * lax.rsqrt(jnp.sum(y * y, axis=-1, keepdims=True) + EPS)
        yn = yn * jnp.where(cb < n_q_blocks, qscale, 1.0)
        y = jnp.where(cb < n_qk_blocks, yn, y)
        o_ref[tile_rows(r), :] = y.astype(o_ref.dtype)
        return carry

    lax.fori_loop(0, n_rows, body, 0, unroll=unroll)


def _conv_call(proj_main, conv_w, col0, row0, batch, seq, width, vertical, heads, dk):
    ch = conv_w.shape[-1]
    nb = ch // LANES
    assert dk == LANES and col0 % LANES == 0 and seq % width == 0
    return pl.pallas_call(
        functools.partial(_conv_kernel, width=width, vertical=vertical, n_qk_blocks=2 * heads, n_q_blocks=heads,
                          qscale=dk ** -0.5),
        grid=(batch, nb),
        in_specs=[pl.BlockSpec((seq, LANES), lambda b, c: (row0 + b, col0 // LANES + c)),
                  pl.BlockSpec((9, LANES), lambda b, c: (0, c))],
        out_specs=pl.BlockSpec((seq, LANES), lambda b, c: (b, c)),
        out_shape=SDS((batch * seq, ch), BF16),
        scratch_shapes=[pltpu.VMEM((3, seq + 2 * width, LANES), F32)],
        compiler_params=_params(("arbitrary", "arbitrary")),
        name="gdn_conv",
    )(proj_main, conv_w.reshape(9, ch))


def _gdn_iter(cf, cb, first, hg, n_ab, q_ref, k_ref, v_ref, lr_ref, arow_ref, dtrow_ref, nw_ref, o_ref, st_ref,
              oacc_ref, sel_ref):
    d = LANES
    chains = [(s, hh) for s in range(2) for hh in range(hg)]
    rows = [pl.ds(pl.multiple_of(c * CHUNK, CHUNK), CHUNK) for c in (cf, cb)]
    masks = [_tri_masks(False), _tri_masks(True)]
    tri = [m[0].astype(BF16) for m in masks]
    lane = lax.broadcasted_iota(jnp.int32, (CHUNK, d), 1)
    strict_w = [jnp.logical_and(lane < CHUNK, r_c) for r_c in
                (lax.broadcasted_iota(jnp.int32, (CHUNK, d), 0) > lane, lax.broadcasted_iota(jnp.int32, (CHUNK, d), 0) < lane)]

    tiles = []
    for s in range(2):
        lr = lr_ref[rows[s], :]
        g_t = -jnp.exp(arow_ref[...]) * _softplus(lr + dtrow_ref[...])
        tiles.append(_split_bf16(jnp.where(lane < n_ab, g_t, _sigmoid(lr))))
    q = [q_ref[rows[s], hh * d:(hh + 1) * d] for s, hh in chains]
    k = [k_ref[rows[s], hh * d:(hh + 1) * d] for s, hh in chains]
    v = [v_ref[rows[s], hh * d:(hh + 1) * d].astype(F32) for s, hh in chains]
    kf = [x.astype(F32) for x in k]

    sel = [jnp.dot(tiles[s][0], sel_ref[g], preferred_element_type=F32)
           + jnp.dot(tiles[s][1], sel_ref[g], preferred_element_type=F32) for g, (s, hh) in enumerate(chains)]
    qkk = [lax.dot_general(jnp.concatenate([q[g], k[g]], axis=0), k[g], _NT, preferred_element_type=F32)
           for g in range(len(chains))]
    gb = [x[:, :d] for x in sel]
    beta = [x[:, d:] for x in sel]
    tr = [_tri_dot(tri[s], jnp.concatenate([gb[g], jnp.where(strict_w[s], gb[g], 0.0)], axis=1))
          for g, (s, hh) in enumerate(chains)]
    cum = [x[:, :d] for x in tr]
    gamma = [jnp.where(masks[s][0], jnp.exp(jnp.where(masks[s][0], tr[g][:, d:d + CHUNK], 0.0)), 0.0)
             for g, (s, hh) in enumerate(chains)]
    qk = [qkk[g][:CHUNK] * gamma[g] for g in range(len(chains))]
    m = [jnp.where(masks[s][1], beta[g][:, :CHUNK] * qkk[g][CHUNK:] * gamma[g], 0.0)
         for g, (s, hh) in enumerate(chains)]
    y = [-x for x in m]
    mb = [x.astype(BF16) for x in m]
    p = [jnp.dot(x, x, preferred_element_type=F32) for x in mb]
    steps = (CHUNK - 1).bit_length() - 1
    for it in range(steps):
        pb = [x.astype(BF16) for x in p]
        if it + 1 < steps:
            yp = [jnp.dot(jnp.concatenate([y[g].astype(BF16), pb[g]], axis=0), pb[g], preferred_element_type=F32)
                  for g in range(len(chains))]
            y = [y[g] + p[g] + yp[g][:CHUNK] for g in range(len(chains))]
            p = [x[CHUNK:] for x in yp]
        else:
            yp = [jnp.dot(y[g].astype(BF16), pb[g], preferred_element_type=F32) for g in range(len(chains))]
            y = [y[g] + p[g] + yp[g] for g in range(len(chains))]
    rhs = [jnp.concatenate([v[g] * beta[g], kf[g] * (beta[g] * jnp.exp(cum[g]))], axis=1) for g in range(len(chains))]
    sol = [rhs[g] + jnp.dot(y[g].astype(BF16), rhs[g].astype(BF16), preferred_element_type=F32)
           for g in range(len(chains))]
    clast = [cum[g][(0 if s else CHUNK - 1):(1 if s else CHUNK), :] for g, (s, hh) in enumerate(chains)]
    qd = [q[g].astype(F32) * jnp.exp(cum[g]) for g in range(len(chains))]
    kd = [(kf[g] * jnp.exp(clast[g] - cum[g])).astype(BF16) for g in range(len(chains))]
    st = [st_ref[s, hh] for s, hh in chains]
    wq = [jnp.dot(jnp.concatenate([sol[g][:, d:].astype(BF16), qd[g].astype(BF16)], axis=0), st[g].astype(BF16),
                  preferred_element_type=F32) for g in range(len(chains))]
    vb = [(sol[g][:, :d] - wq[g][:CHUNK]).astype(BF16) for g in range(len(chains))]
    o = [wq[g][CHUNK:] + jnp.dot(qk[g].astype(BF16), vb[g], preferred_element_type=F32) for g in range(len(chains))]
    st_new = [st[g] * jnp.exp(clast[g]) + lax.dot_general(kd[g], vb[g], _TN, preferred_element_type=F32)
              for g in range(len(chains))]
    for g, (s, hh) in enumerate(chains):
        st_ref[s, hh] = st_new[g]
        cols = slice(hh * d, (hh + 1) * d)
        if first:
            oacc_ref[rows[s], cols] = o[g]
        else:
            tot = oacc_ref[rows[s], cols] + o[g]
            yn = tot * lax.rsqrt(jnp.mean(tot * tot, axis=-1, keepdims=True) + EPS) * nw_ref[...]
            o_ref[rows[s], cols] = yn.astype(o_ref.dtype)


def _gdn_iter_paired(i, n, per, first, hg, n_ab, q_ref, k_ref, v_ref, lr_ref, arow_ref, dtrow_ref, nw_ref, o_ref,
                     st_ref, oacc_ref, sel_ref):
    d = LANES
    chains = [(s, u, hh) for s in range(2) for u in range(per) for hh in range(hg)]
    pairs = [(s, u, 2 * p) for s in range(2) for u in range(per) for p in range(hg // 2)]
    nc, npair = len(chains), len(pairs)
    rows = {(s, u): pl.ds(pl.multiple_of(((n - 1 - (i * per + u)) if s else (i * per + u)) * CHUNK, CHUNK), CHUNK)
            for s in range(2) for u in range(per)}
    tri = [_tri_masks(rev)[0].astype(BF16) for rev in (False, True)]
    lane = lax.broadcasted_iota(jnp.int32, (CHUNK, d), 1)
    row = lax.broadcasted_iota(jnp.int32, (CHUNK, d), 0)
    left = lane < CHUNK
    col = jnp.where(left, lane, lane - CHUNK)
    incl_p = [row >= col, row <= col]
    strict_p = [row > col, row < col]
    zero_b = jnp.zeros((CHUNK, d), BF16)

    def lanes(a, b):
        return jnp.concatenate([a, b], axis=1)

    def stack(a, b):
        return jnp.concatenate([a, b], axis=0)

    def bdiag(x):
        z = jnp.zeros_like(x)
        return stack(jnp.where(left, x, z), jnp.where(left, z, x))

    tiles = {}
    for su in rows:
        lr = lr_ref[rows[su], :]
        g_t = -jnp.exp(arow_ref[...]) * _softplus(lr + dtrow_ref[...])
        tiles[su] = _split_bf16(jnp.where(lane < n_ab, g_t, _sigmoid(lr)))
    q = [q_ref[rows[s, u], hh * d:(hh + 1) * d] for s, u, hh in chains]
    k = [k_ref[rows[s, u], hh * d:(hh + 1) * d] for s, u, hh in chains]
    v = [v_ref[rows[s, u], hh * d:(hh + 1) * d].astype(F32) for s, u, hh in chains]
    kf = [x.astype(F32) for x in k]

    sel = [jnp.dot(tiles[s, u][0], sel_ref[s * hg + hh], preferred_element_type=F32)
           + jnp.dot(tiles[s, u][1], sel_ref[s * hg + hh], preferred_element_type=F32) for s, u, hh in chains]
    qkk = [lax.dot_general(stack(lanes(q[2 * p], q[2 * p + 1]), lanes(k[2 * p], k[2 * p + 1])),
                           stack(lanes(k[2 * p], zero_b), lanes(zero_b, k[2 * p + 1])), _NT,
                           preferred_element_type=F32) for p in range(npair)]
    gb = [x[:, :d] for x in sel]
    beta = [x[:, d:] for x in sel]
    beta_p = [jnp.where(left, beta[2 * p], beta[2 * p + 1]) for p in range(npair)]
    tr = [_tri_dot(tri[s], jnp.concatenate(
        [gb[2 * p], gb[2 * p + 1], jnp.where(strict_p[s], jnp.where(left, gb[2 * p], gb[2 * p + 1]), 0.0)], axis=1))
        for p, (s, u, hh) in enumerate(pairs)]
    cum = [tr[g // 2][:, (g % 2) * d:(g % 2 + 1) * d] for g in range(nc)]
    gamma = [jnp.where(incl_p[s], jnp.exp(jnp.where(incl_p[s], tr[p][:, 2 * d:], 0.0)), 0.0)
             for p, (s, u, hh) in enumerate(pairs)]
    qk = [qkk[p][:CHUNK] * gamma[p] for p in range(npair)]
    m = [jnp.where(strict_p[s], beta_p[p] * qkk[p][CHUNK:] * gamma[p], 0.0) for p, (s, u, hh) in enumerate(pairs)]
    y = [-x for x in m]
    mb = [x.astype(BF16) for x in m]
    pw = [jnp.dot(x, bdiag(x), preferred_element_type=F32) for x in mb]
    steps = (CHUNK - 1).bit_length() - 1
    for it in range(steps):
        pb = [x.astype(BF16) for x in pw]
        if it + 1 < steps:
            yp = [jnp.dot(stack(y[p].astype(BF16), pb[p]), bdiag(pb[p]), preferred_element_type=F32)
                  for p in range(npair)]
            y = [y[p] + pw[p] + yp[p][:CHUNK] for p in range(npair)]
            pw = [x[CHUNK:] for x in yp]
        else:
            yp = [jnp.dot(y[p].astype(BF16), bdiag(pb[p]), preferred_element_type=F32) for p in range(npair)]
            y = [y[p] + pw[p] + yp[p] for p in range(npair)]
    rhs = [lanes(v[g] * beta[g], kf[g] * (beta[g] * jnp.exp(cum[g]))) for g in range(nc)]
    zero_r = jnp.zeros((CHUNK, 2 * d), BF16)
    ysol = [jnp.dot(y[p].astype(BF16), stack(lanes(rhs[2 * p].astype(BF16), zero_r), lanes(zero_r, rhs[2 * p + 1].astype(BF16))),
                    preferred_element_type=F32) for p in range(npair)]
    sol = [rhs[g] + ysol[g // 2][:, (g % 2) * 2 * d:(g % 2 + 1) * 2 * d] for g in range(nc)]
    clast = [cum[g][(0 if s else CHUNK - 1):(1 if s else CHUNK), :] for g, (s, u, hh) in enumerate(chains)]
    qd = [(q[g].astype(F32) * jnp.exp(cum[g])).astype(BF16) for g in range(nc)]
    kd = [(kf[g] * jnp.exp(clast[g] - cum[g])).astype(BF16) for g in range(nc)]
    wb = [sol[g][:, d:].astype(BF16) for g in range(nc)]
    dec = [jnp.exp(clast[g]) for g in range(nc)]
    st = {(s, hh): st_ref[s, hh] for s in range(2) for hh in range(hg)}
    for u in range(per):
        gs = [g for g, c in enumerate(chains) if c[1] == u]
        ps = [p for p, c in enumerate(pairs) if c[1] == u]
        wq = {g: jnp.dot(stack(wb[g], qd[g]), st[chains[g][0], chains[g][2]].astype(BF16),
                         preferred_element_type=F32) for g in gs}
        vb = {g: (sol[g][:, :d] - wq[g][:CHUNK]).astype(BF16) for g in gs}
        oq = {p: jnp.dot(qk[p].astype(BF16), stack(lanes(vb[2 * p], zero_b), lanes(zero_b, vb[2 * p + 1])),
                         preferred_element_type=F32) for p in ps}
        for g in gs:
            s, _, hh = chains[g]
            st[s, hh] = st[s, hh] * dec[g] + lax.dot_general(kd[g], vb[g], _TN, preferred_element_type=F32)
            o = wq[g][CHUNK:] + oq[g // 2][:, (g % 2) * d:(g % 2 + 1) * d]
            cols = slice(hh * d, (hh + 1) * d)
            if first:
                oacc_ref[rows[s, u], cols] = o
            else:
                tot = oacc_ref[rows[s, u], cols] + o
                yn = tot * lax.rsqrt(jnp.mean(tot * tot, axis=-1, keepdims=True) + EPS) * nw_ref[...]
                o_ref[rows[s, u], cols] = yn.astype(o_ref.dtype)
    for (s, hh), val in st.items():
        st_ref[s, hh] = val


def _gdn_kernel(*refs, hg, per_chunks, a_off, n_heads, has_init, want_state):
    q_ref, k_ref, v_ref, lr_ref, arow_ref, dtrow_ref, nw_ref = refs[:7]
    refs = refs[7:]
    if has_init:
        s0_ref, refs = refs[0], refs[1:]
    o_ref, refs = refs[0], refs[1:]
    if want_state:
        sfin_ref, refs = refs[0], refs[1:]
    st_ref, oacc_ref, sel_ref = refs
    n = q_ref.shape[0] // CHUNK
    if has_init:
        st_ref[...] = s0_ref[...]
    else:
        st_ref[...] = jnp.zeros(st_ref.shape, F32)
    grp = pl.program_id(1)
    r = lax.broadcasted_iota(jnp.int32, sel_ref.shape[1:], 0)
    c = lax.broadcasted_iota(jnp.int32, sel_ref.shape[1:], 1)
    for s in range(2):
        for hh in range(hg):
            head = grp * hg + hh
            col_a = a_off + s * n_heads + head
            col_b = a_off + (2 + s) * n_heads + head
            sel_ref[s * hg + hh] = jnp.where(r == jnp.where(c < LANES, col_a, col_b), 1.0, 0.0).astype(BF16)
    common = (hg, a_off + 2 * n_heads, q_ref, k_ref, v_ref, lr_ref, arow_ref, dtrow_ref, nw_ref, o_ref, st_ref,
              oacc_ref, sel_ref)

    paired = hg % 2 == 0
    per = per_chunks if (paired and n % (2 * per_chunks) == 0) else 1

    def half(first):
        def body(i, carry):
            if paired:
                _gdn_iter_paired(i, n, per, first, *common)
            else:
                _gdn_iter(i, n - 1 - i, first, *common)
            return carry
        return body

    lax.fori_loop(0, n // (2 * per), half(True), 0)
    lax.fori_loop(n // (2 * per), n // per, half(False), 0)
    if want_state:
        sfin_ref[...] = st_ref[...]


def _gdn_call(conv, proj_small, a_log, dt_bias, norm_w, state, layer, row0, batch, seq, heads, hg, a_off,
              per_chunks=1):
    has_init = state is not None
    want_state = not has_init
    d = LANES
    ng = heads // hg
    small_w = proj_small.shape[1]
    assert heads % hg == 0 and seq % (2 * CHUNK) == 0 and small_w == LANES and a_off + 4 * heads <= small_w
    pad = lambda x: jnp.pad(x.reshape(1, 2 * heads), ((0, 0), (a_off, small_w - a_off - 2 * heads)))
    wide = seq * hg * d
    fits = (3 * 2 * 2 + 2 * 2 + 4) * wide <= VMEM_LIMIT * 3 // 4
    one = {} if fits else dict(pipeline_mode=pl.Buffered(1))
    in_specs = [pl.BlockSpec((seq, hg * d), lambda b, g: (b, g), **one),
                pl.BlockSpec((seq, hg * d), lambda b, g: (b, ng + g), **one),
                pl.BlockSpec((seq, hg * d), lambda b, g: (b, 2 * ng + g), **one),
                pl.BlockSpec((seq, small_w), lambda b, g: (row0 + b, 0), **one),
                pl.BlockSpec((1, small_w), lambda b, g: (0, 0)),
                pl.BlockSpec((1, small_w), lambda b, g: (0, 0)),
                pl.BlockSpec((1, d), lambda b, g: (0, 0))]
    args = [conv, conv, conv, proj_small, pad(a_log), pad(dt_bias), norm_w.reshape(1, d)]
    if has_init:
        in_specs.append(pl.BlockSpec((None, None, 2, hg, d, d), lambda b, g: (b, layer, 0, g, 0, 0), **one))
        args.append(state)
    out_specs = [pl.BlockSpec((seq, hg * d), lambda b, g: (b, g), **one)]
    out_shape = [SDS((batch * seq, heads * d), BF16)]
    if want_state:
        out_specs.append(pl.BlockSpec((None, 2, hg, d, d), lambda b, g: (b, 0, g, 0, 0)))
        out_shape.append(SDS((batch, 2, heads, d, d), F32))
    outs = pl.pallas_call(
        functools.partial(_gdn_kernel, hg=hg, per_chunks=per_chunks, a_off=a_off, n_heads=heads, has_init=has_init,
                          want_state=want_state),
        grid=(batch, ng),
        in_specs=in_specs, out_specs=out_specs, out_shape=out_shape,
        scratch_shapes=[pltpu.VMEM((2, hg, d, d), F32), pltpu.VMEM((seq, hg * d), F32),
                        pltpu.VMEM((2 * hg, LANES, 2 * LANES), BF16)],
        compiler_params=_params(("arbitrary", "arbitrary"), VMEM_LIMIT if fits else VMEM_LIMIT_MAX),
        name="gdn",
    )(*args)
    return outs if want_state else (outs[0], None)


def _out_proj_kernel(gl_ref, gc_ref, dl_ref, dc_ref, gr_ref, dz_ref, x_ref, gate_ref, w_ref, o_ref, lhs_ref,
                     *, n_lat_blocks):
    i = pl.program_id(0)
    half = gl_ref.shape[1]

    def fill(g_ref, d_ref):
        lhs_ref[:, :half] = (g_ref[...].astype(F32) * _silu(gr_ref[...].astype(F32))).astype(BF16)
        lhs_ref[:, half:] = (d_ref[...].astype(F32) * _silu(dz_ref[...].astype(F32))).astype(BF16)

    @pl.when(jnp.logical_and(pl.program_id(1) == 0, i < n_lat_blocks))
    def _():
        fill(gl_ref, dl_ref)

    @pl.when(jnp.logical_and(pl.program_id(1) == 0, i >= n_lat_blocks))
    def _():
        fill(gc_ref, dc_ref)

    mix = jnp.dot(lhs_ref[...], w_ref[...], preferred_element_type=F32)
    o_ref[...] = x_ref[...] + gate_ref[...] * mix


def _out_proj_call(og_lat, og_ctx, od_lat, od_ctx, proj_main, gr_col, dz_col, x, mod3, w_out, layer, mod_row, tm, tn):
    t, d = x.shape
    half = og_lat.shape[1]
    nl = og_lat.shape[0] // tm
    nc = og_ctx.shape[0] // tm
    assert gr_col % half == 0 and dz_col % half == 0 and nl * tm == og_lat.shape[0] and nc * tm == og_ctx.shape[0]
    lat_map = lambda i, j: (jnp.minimum(i, nl - 1), 0)
    ctx_map = lambda i, j: (jnp.clip(i - nl, 0, nc - 1), 0)
    return pl.pallas_call(
        functools.partial(_out_proj_kernel, n_lat_blocks=nl),
        grid=(t // tm, d // tn),
        in_specs=[pl.BlockSpec((tm, half), lat_map), pl.BlockSpec((tm, half), ctx_map),
                  pl.BlockSpec((tm, half), lat_map), pl.BlockSpec((tm, half), ctx_map),
                  pl.BlockSpec((tm, half), lambda i, j: (i, gr_col // half)),
                  pl.BlockSpec((tm, half), lambda i, j: (i, dz_col // half)),
                  pl.BlockSpec((tm, tn), lambda i, j: (i, j)),
                  pl.BlockSpec((None, 1, tn), lambda i, j: (mod_row(i, tm) * N_MOD + 2, 0, j)),
                  pl.BlockSpec((None, 2 * half, tn), lambda i, j: (layer, 0, j))],
        out_specs=pl.BlockSpec((tm, tn), lambda i, j: (i, j)),
        out_shape=SDS((t, d), F32),
        scratch_shapes=[pltpu.VMEM((tm, 2 * half), BF16)],
        compiler_params=_params(("arbitrary", "arbitrary")),
        name="out_proj",
    )(og_lat, og_ctx, od_lat, od_ctx, proj_main, proj_main, x, mod3, w_out)


def _pick4(sel, vals):
    return jnp.where(sel == 0, vals[0], jnp.where(sel == 1, vals[1], jnp.where(sel == 2, vals[2], vals[3])))


def _router_kernel(x_ref, ln_ref, shift_ref, scale_ref, rw_ref, rb_ref, hp_ref, idx_ref, gate_ref, *, n_experts):
    x = x_ref[...]
    y = x * lax.rsqrt(jnp.mean(x * x, axis=-1, keepdims=True) + EPS) * ln_ref[...]
    h = y * (1.0 + scale_ref[...]) + shift_ref[...]
    hb = h.astype(BF16)
    half = h.shape[1] // 2
    bits = pltpu.bitcast(hb.astype(F32), jnp.uint32)
    hp_ref[...] = lax.shift_right_logical(bits[:, :half], jnp.uint32(16)) | bits[:, half:]
    h_hi, h_lo = hb, (h - hb.astype(F32)).astype(BF16)
    w_hi, w_lo = _split_bf16(rw_ref[...])
    logits = (lax.dot_general(w_hi, h_hi, _NT, preferred_element_type=F32)
              + lax.dot_general(w_lo, h_hi, _NT, preferred_element_type=F32)
              + lax.dot_general(w_hi, h_lo, _NT, preferred_element_type=F32))
    mx = jnp.max(logits, axis=0, keepdims=True)
    ex = jnp.exp(logits - mx)
    probs = ex / jnp.sum(ex, axis=0, keepdims=True)
    sel = probs + rb_ref[...]
    per = n_experts // N_GROUPS
    assert per == 4 and TOP_K == 2
    rows_s = [sel[e:e + 1, :] for e in range(n_experts)]
    rows_p = [probs[e:e + 1, :] for e in range(n_experts)]
    best = None
    best_score = None
    for g in range(N_GROUPS):
        r = rows_s[g * per:(g + 1) * per]
        score = None
        for a in range(per):
            for b in range(a + 1, per):
                pair = r[a] + r[b]
                score = pair if score is None else jnp.maximum(score, pair)
        if g == 0:
            best, best_score = jnp.zeros_like(score, dtype=jnp.int32), score
        else:
            better = score > best_score
            best = jnp.where(better, g, best)
            best_score = jnp.where(better, score, best_score)
    sg = [_pick4(best, [rows_s[g * per + a] for g in range(N_GROUPS)]) for a in range(per)]
    pg = [_pick4(best, [rows_p[g * per + a] for g in range(N_GROUPS)]) for a in range(per)]
    i1 = jnp.zeros_like(best)
    m1 = sg[0]
    for a in range(1, per):
        better = sg[a] > m1
        i1 = jnp.where(better, a, i1)
        m1 = jnp.where(better, sg[a], m1)
    i2 = None
    m2 = None
    for a in range(per):
        cand = jnp.where(i1 == a, -jnp.inf, sg[a])
        if i2 is None:
            i2, m2 = jnp.zeros_like(best), cand
        else:
            better = cand > m2
            i2 = jnp.where(better, a, i2)
            m2 = jnp.where(better, cand, m2)
    p1 = _pick4(i1, pg)
    p2 = _pick4(i2, pg)
    tot = p1 + p2
    idx_ref[0:1, :] = best * per + i1
    idx_ref[1:2, :] = best * per + i2
    gate_ref[0:1, :] = p1 / tot
    gate_ref[1:2, :] = p2 / tot


def _router_call(x, ln_w, mod3, router_w, router_bias, mod_row, tm):
    t, d = x.shape
    e = router_w.shape[1]
    return pl.pallas_call(
        functools.partial(_router_kernel, n_experts=e),
        grid=(t // tm,),
        in_specs=[pl.BlockSpec((tm, d), lambda i: (i, 0)),
                  pl.BlockSpec((1, d), lambda i: (0, 0)),
                  pl.BlockSpec((None, 1, d), lambda i: (mod_row(i, tm) * N_MOD + 3, 0, 0)),
                  pl.BlockSpec((None, 1, d), lambda i: (mod_row(i, tm) * N_MOD + 4, 0, 0)),
                  pl.BlockSpec((e, d), lambda i: (0, 0)),
                  pl.BlockSpec((e, 1), lambda i: (0, 0))],
        out_specs=[pl.BlockSpec((tm, d // 2), lambda i: (i, 0)),
                   pl.BlockSpec((TOP_K, tm), lambda i: (0, i)),
                   pl.BlockSpec((TOP_K, tm), lambda i: (0, i))],
        out_shape=[SDS((t, d // 2), jnp.uint32), SDS((TOP_K, t), jnp.int32), SDS((TOP_K, t), F32)],
        compiler_params=_params(("arbitrary",)),
        name="router",
    )(x, ln_w.reshape(1, d), mod3, mod3, router_w.T, router_bias.reshape(e, 1))


def _moe_kernel(be_ref, nused_ref, tok0_ref, tokn_ref, dst_ref, hp_ref, w1_ref, w3_ref, w2_ref, g_ref, y_ref,
                xbuf_ref, xb_ref, act_ref, obuf_ref, gsem, ssem):
    i = pl.program_id(0)
    s = pl.program_id(1)
    n_steps = pl.num_programs(1)
    n_up, rows, tf = act_ref.shape
    tn = w2_ref.shape[1]
    n_down = obuf_ref.shape[1] // tn
    nused = nused_ref[0]
    used = i < nused
    slot = lax.rem(i, 2)
    g_per = rows // (n_up + n_down)
    s_per = rows // n_up

    def gather(tab_ref, r, sl):
        return pltpu.make_async_copy(hp_ref.at[pl.ds(tab_ref[0, r], 1), :], xbuf_ref.at[sl, pl.ds(r, 1), :], gsem.at[sl])

    def scatter(r):
        return pltpu.make_async_copy(obuf_ref.at[pl.ds(r, 1), :], y_ref.at[pl.ds(dst_ref[0, r], 1), :], ssem)

    def gather_wait(sl):
        pltpu.make_async_copy(hp_ref.at[pl.ds(0, rows), :], xbuf_ref.at[sl], gsem.at[sl]).wait()

    def scatter_wait():
        pltpu.make_async_copy(obuf_ref, y_ref.at[pl.ds(0, rows), :], ssem).wait()

    def gather_next():
        for r in range(g_per):
            gather(tokn_ref, s * g_per + r, 1 - slot).start()

    @pl.when(jnp.logical_and(i == 0, s == 0))
    def _():
        obuf_ref[...] = jnp.zeros(obuf_ref.shape, F32)

        def start(r, carry):
            gather(tok0_ref, r, 0).start()
            return carry

        lax.fori_loop(0, rows, start, 0, unroll=8)

    @pl.when(jnp.logical_and(used, s == 0))
    def _():
        gather_wait(slot)
        words = xbuf_ref[slot]
        half = words.shape[1]
        lo = pltpu.bitcast(lax.shift_left(words, jnp.uint32(16)), F32)
        hi = pltpu.bitcast(words & jnp.uint32(0xFFFF0000), F32)
        xb_ref[:, :half] = lo.astype(BF16)
        xb_ref[:, half:] = hi.astype(BF16)

    @pl.when(jnp.logical_and(used, s < n_up))
    def _():
        xb = xb_ref[...]
        a = jnp.dot(xb, w1_ref[...], preferred_element_type=F32)
        b = jnp.dot(xb, w3_ref[...], preferred_element_type=F32)
        act_ref[s] = (_silu(a) * b).astype(BF16)
        gather_next()
        for r in range(s_per):
            scatter(s * s_per + r).start()

    @pl.when(jnp.logical_and(used, s == n_up))
    def _():
        scatter_wait()

    @pl.when(jnp.logical_and(used, s >= n_up))
    def _():
        acc = jnp.dot(act_ref[0], w2_ref[0:tf, :], preferred_element_type=F32)
        for f in range(1, n_up):
            acc = acc + jnp.dot(act_ref[f], w2_ref[f * tf:(f + 1) * tf, :], preferred_element_type=F32)
        val = acc * g_ref[...]
        gather_next()
        obuf_ref[:, pl.ds(pl.multiple_of((s - n_up) * tn, tn), tn)] = val

    @pl.when(jnp.logical_and(i == nused, s == 0))
    def _():
        gather_wait(slot)

        def start(r, carry):
            scatter(r).start()
            return carry

        lax.fori_loop(0, rows, start, 0, unroll=8)
        scatter_wait()


def _moe_call(block_e, nused, slot_tok, slot_dst, slot_gate, hp, w1, w3, w2, layer, n_out_rows, rows, tf, tn):
    _, n_e, d, dff = w1.shape
    nb = slot_tok.shape[0] // rows
    n_up, n_down = dff // tf, d // tn
    assert rows % (n_up + n_down) == 0 and rows % n_up == 0 and slot_dst.shape[0] == (nb + 1) * rows

    def blk(i, nu):
        return jnp.minimum(i, nu[0] - 1)

    def up_map(i, s, be, nu):
        return (layer, be[blk(i, nu)], 0, jnp.where(i < nu[0], jnp.minimum(s, n_up - 1), n_up - 1))

    def down_map(i, s, be, nu):
        return (layer, be[blk(i, nu)], 0, jnp.where(i < nu[0], jnp.maximum(s - n_up, 0), n_down - 1))

    smem_tab = lambda f: pl.BlockSpec((None, 1, rows), f, memory_space=pltpu.SMEM)
    return pl.pallas_call(
        _moe_kernel,
        grid_spec=pltpu.PrefetchScalarGridSpec(
            num_scalar_prefetch=2,
            grid=(nb, n_up + n_down),
            in_specs=[smem_tab(lambda i, s, be, nu: (0, 0, 0)),
                      smem_tab(lambda i, s, be, nu: (jnp.minimum(i + 1, nb - 1), 0, 0)),
                      smem_tab(lambda i, s, be, nu: (i, 0, 0)),
                      pl.BlockSpec(memory_space=pl.ANY),
                      pl.BlockSpec((None, None, d, tf), up_map),
                      pl.BlockSpec((None, None, d, tf), up_map),
                      pl.BlockSpec((None, None, dff, tn), down_map),
                      pl.BlockSpec((rows, 1), lambda i, s, be, nu: (blk(i, nu), 0))],
            out_specs=pl.BlockSpec(memory_space=pl.ANY),
            scratch_shapes=[pltpu.VMEM((2, rows, d // 2), jnp.uint32), pltpu.VMEM((rows, d), BF16),
                            pltpu.VMEM((n_up, rows, tf), BF16), pltpu.VMEM((rows, d), F32),
                            pltpu.SemaphoreType.DMA((2,)), pltpu.SemaphoreType.DMA(())]),
        out_shape=SDS((n_out_rows, d), F32),
        compiler_params=_params(("arbitrary", "arbitrary")),
        name="moe_ffn",
    )(block_e, nused, slot_tok.reshape(nb, 1, rows), slot_tok.reshape(nb, 1, rows),
      slot_dst.reshape(nb + 1, 1, rows), hp, w1, w3, w2, slot_gate.reshape(nb * rows, 1))


def _combine_kernel(x_ref, gate_ref, fw_ref, y0_ref, y1_ref, *o_refs, n_lat_blocks):
    v = x_ref[...] + gate_ref[...] * (y0_ref[...] + y1_ref[...])
    if len(o_refs) == 1:
        o_refs[0][...] = v
        return
    v = v * lax.rsqrt(jnp.mean(v * v, axis=-1, keepdims=True) + EPS) * fw_ref[...]

    @pl.when(pl.program_id(0) < n_lat_blocks)
    def _():
        o_refs[0][...] = v

    @pl.when(pl.program_id(0) >= n_lat_blocks)
    def _():
        o_refs[1][...] = v


def _combine_call(x, mod3, final_w, ys, mod_row, tm, t_lat, final):
    t, d = x.shape
    nl = t_lat // tm
    nc = (t - t_lat) // tm
    if final:
        out_specs = [pl.BlockSpec((tm, d), lambda i: (jnp.minimum(i, nl - 1), 0)),
                     pl.BlockSpec((tm, d), lambda i: (jnp.maximum(i - nl, 0), 0))]
        out_shape = [SDS((t_lat, d), F32), SDS((t - t_lat, d), F32)]
    else:
        out_specs = [pl.BlockSpec((tm, d), lambda i: (i, 0))]
        out_shape = [SDS((t, d), F32)]
    assert nl * tm == t_lat and (nl + nc) * tm == t
    outs = pl.pallas_call(
        functools.partial(_combine_kernel, n_lat_blocks=nl),
        grid=(t // tm,),
        in_specs=[pl.BlockSpec((tm, d), lambda i: (i, 0)),
                  pl.BlockSpec((None, 1, d), lambda i: (mod_row(i, tm) * N_MOD + 5, 0, 0)),
                  pl.BlockSpec((1, d), lambda i: (0, 0)),
                  pl.BlockSpec((tm, d), lambda i: (i, 0)),
                  pl.BlockSpec((tm, d), lambda i: (t // tm + i, 0))],
        out_specs=out_specs, out_shape=out_shape,
        compiler_params=_params(("arbitrary",)),
        name="moe_combine",
    )(x, mod3, final_w.reshape(1, d), ys, ys)
    return outs if final else outs[0]


def _dispatch_tables(idx, gate, n_experts, rows):
    t = idx.shape[1]
    n_assign = t * TOP_K
    flat_e = idx.T.reshape(-1)
    flat_g = gate.T.reshape(-1)
    order = jnp.argsort(flat_e, stable=True).astype(jnp.int32)
    counts = jnp.sum((flat_e[:, None] == jnp.arange(n_experts, dtype=jnp.int32)[None, :]).astype(jnp.int32), axis=0)
    padded = (counts + rows - 1) // rows * rows
    pad_end = jnp.cumsum(padded)
    pad_start = pad_end - padded
    start = jnp.cumsum(counts) - counts
    n_blocks = -(-n_assign // rows) + n_experts
    n_slots = n_blocks * rows
    block_e = jnp.minimum(jnp.searchsorted(pad_end, jnp.arange(n_blocks, dtype=jnp.int32) * rows, side='right'),
                          n_experts - 1).astype(jnp.int32)
    nused = (pad_end[-1] // rows).astype(jnp.int32).reshape(1)
    slot = jnp.arange(n_slots, dtype=jnp.int32)
    slot_e = jnp.repeat(block_e, rows)
    within = slot - pad_start[slot_e]
    valid = jnp.logical_and(within < counts[slot_e], slot < pad_end[-1])
    src = order[jnp.clip(start[slot_e] + within, 0, n_assign - 1)]
    slot_tok = jnp.where(valid, src // TOP_K, 0).astype(jnp.int32)
    slot_gate = jnp.where(valid, flat_g[src], 0.0)
    dump = n_assign + slot % rows
    slot_dst = jnp.where(valid, (src % TOP_K) * t + src // TOP_K, dump).astype(jnp.int32)
    slot_dst = jnp.concatenate([n_assign + jnp.arange(rows, dtype=jnp.int32), slot_dst])
    return block_e, nused, slot_tok, slot_dst, slot_gate


def _tile(n, target):
    if n <= target:
        return n
    best = None
    for cand in range(LANES, target + 1, LANES):
        if n % cand == 0:
            best = cand
    assert best is not None, (n, target)
    return best


def kernel(x_prompt, x_sample, state_gla, state_gdn, c, c_ctx, ln1_w, w_mod, b_mod, w_in, gla_w2, gla_b2, gla_norm_w, gdn_conv_w, gdn_a_log, gdn_dt_bias, gdn_norm_w, w_out, ln2_w, router_w, router_bias, w1, w3, w2, final_norm_w):
    bc, lc, d = x_prompt.shape
    bl, ll, _ = x_sample.shape
    depth = w_in.shape[0]
    gh, gdk, gdv = state_gla.shape[3:]
    dh, ddk, ddv = state_gdn.shape[3:]
    rank = gla_w2.shape[2]
    n_experts = router_w.shape[1]
    qk_w, d_gla, d_gdn = gh * gdk, gh * gdv, dh * ddv
    t_lat, t_ctx = bl * ll, bc * lc
    assert ddk == LANES and ddv == LANES and t_lat % t_ctx == 0 and ll % GRID_W == 0

    x = jnp.concatenate([x_sample.reshape(t_lat, d), x_prompt.reshape(t_ctx, d)], axis=0)
    n_cond = -(-(bl + 1) // 8) * 8
    cond = jnp.concatenate([c, c_ctx[None, :], jnp.zeros((n_cond - bl - 1, d), F32)], axis=0)
    mod = _mod_call(cond, w_mod, b_mod)

    def mod_row(i, tm):
        return jnp.minimum(i * tm // ll, bl)

    o_glr = 2 * qk_w + 2 * d_gla
    o_dqkv = o_glr + 2 * rank
    o_da = o_dqkv + 4 * d_gdn
    n_small = 2 * rank + 4 * dh
    small_w = -(-n_small // LANES) * LANES
    gr_col = 2 * qk_w + d_gla
    dqkv_col = gr_col + d_gla
    dz_col = dqkv_col + 3 * d_gdn

    tm = math.gcd(math.gcd(ll, t_ctx), 512)
    w_main = jnp.concatenate([w_in[:, :, :o_glr], w_in[:, :, o_dqkv:o_da]], axis=2).astype(BF16)
    w_small = jnp.concatenate([w_in[:, :, o_glr:o_dqkv], w_in[:, :, o_da:],
                               jnp.zeros((depth, d, small_w - n_small), F32)], axis=2).astype(BF16)
    w_out_b, w1_b, w3_b, w2_b = (w.astype(BF16) for w in (w_out, w1, w3, w2))
    gla_states, gdn_states = [], []
    for l in range(depth):
        mod3 = mod[l].reshape(n_cond * N_MOD, 1, d)
        proj_main, proj_small = _in_proj_call(x, ln1_w[l], mod3, w_main, w_small, l, mod_row, tm,
                                              _tile(w_main.shape[2], 1024))

        og_lat, _ = _gla_call(proj_main, proj_small, gla_w2[l], gla_b2[l], gla_norm_w[l], state_gla, l,
                              0, bl, ll, gh, gdk, gdv, rank)
        og_ctx, sg = _gla_call(proj_main, proj_small, gla_w2[l], gla_b2[l], gla_norm_w[l], None, l,
                               t_lat // lc, bc, lc, gh, gdk, gdv, rank)
        gla_states.append(sg)

        conv_lat = _conv_call(proj_main, gdn_conv_w[l], dqkv_col, 0, bl, ll, GRID_W, True, dh, ddk)
        conv_ctx = _conv_call(proj_main, gdn_conv_w[l], dqkv_col, t_lat // t_ctx, 1, t_ctx, lc, False, dh, ddk)
        hg = math.gcd(dh, 8)
        od_lat, _ = _gdn_call(conv_lat, proj_small, gdn_a_log[l], gdn_dt_bias[l], gdn_norm_w[l], state_gdn, l,
                              0, bl, ll, dh, hg, 2 * rank, per_chunks=2)
        od_ctx, sd = _gdn_call(conv_ctx, proj_small, gdn_a_log[l], gdn_dt_bias[l], gdn_norm_w[l], None, l,
                               t_lat // lc, bc, lc, dh, hg, 2 * rank, per_chunks=2)
        gdn_states.append(sd)

        x1 = _out_proj_call(og_lat, og_ctx, od_lat, od_ctx, proj_main, gr_col, dz_col, x, mod3,
                            w_out_b, l, mod_row, tm, _tile(d, 1024))

        hp, idx, gate = _router_call(x1, ln2_w[l], mod3, router_w, router_bias, mod_row, tm)
        block_e, nused, slot_tok, slot_dst, slot_gate = _dispatch_tables(idx, gate, n_experts, MOE_ROWS)
        ys = _moe_call(block_e, nused, slot_tok, slot_dst, slot_gate, hp, w1_b, w3_b, w2_b, l,
                       TOP_K * (t_lat + t_ctx) + MOE_ROWS, MOE_ROWS, _tile(w1.shape[-1], 512), _tile(d, 1024))
        x = _combine_call(x1, mod3, final_norm_w, ys, mod_row, _tile(tm, 256), t_lat, l == depth - 1)

    y_sample = x[0].reshape(bl, ll, d)
    y_prompt = x[1].reshape(bc, lc, d)
    return (y_prompt, y_sample, jnp.stack(gla_states, axis=1), jnp.stack(gdn_states, axis=1))
```

```python
import functools
import math

import jax
import jax.numpy as jnp
from jax import lax
from jax.experimental import pallas as pl
from jax.experimental.pallas import tpu as pltpu

F32 = jnp.float32
BF16 = jnp.bfloat16
SDS = jax.ShapeDtypeStruct

EPS = 1e-6
CHUNK = 64
GRID_W = 64
GLA_TAU = 16.0
N_GROUPS = 4
TOP_K = 2
N_MOD = 6
LANES = 128
MOE_ROWS = 512
VMEM_LIMIT = 56 * 1024 * 1024
VMEM_LIMIT_MAX = 60 * 1024 * 1024

_NT = (((1,), (1,)), ((), ()))
_TN = (((0,), (0,)), ((), ()))


def _params(sem, vmem=VMEM_LIMIT):
    return pltpu.CompilerParams(dimension_semantics=sem, vmem_limit_bytes=vmem)


def _sigmoid(x):
    return 1.0 / (1.0 + jnp.exp(-x))


def _silu(x):
    return x * _sigmoid(x)


def _softplus(x):
    return jnp.maximum(x, 0.0) + jnp.log1p(jnp.exp(-jnp.abs(x)))


def _split_bf16(x):
    hi = x.astype(BF16)
    lo = (x - hi.astype(F32)).astype(BF16)
    return hi, lo


def _tri_dot(tri, x):
    hi, lo = _split_bf16(x)
    return (jnp.dot(tri, hi, preferred_element_type=F32)
            + jnp.dot(tri, lo, preferred_element_type=F32))


def _tri_masks(rev):
    r = lax.broadcasted_iota(jnp.int32, (CHUNK, CHUNK), 0)
    c = lax.broadcasted_iota(jnp.int32, (CHUNK, CHUNK), 1)
    if rev:
        return r <= c, r < c
    return r >= c, r > c


def _mod_kernel(c_ref, w_ref, b_ref, o_ref):
    a = _silu(c_ref[...]).astype(BF16)
    o_ref[...] = jnp.dot(a, w_ref[...].astype(BF16), preferred_element_type=F32) + b_ref[...]


def _mod_call(cond, w_mod, b_mod):
    depth, d, n = w_mod.shape
    r = cond.shape[0]
    tn = 512
    return pl.pallas_call(
        _mod_kernel,
        grid=(depth, n // tn),
        in_specs=[pl.BlockSpec((r, d), lambda l, j: (0, 0)),
                  pl.BlockSpec((None, d, tn), lambda l, j: (l, 0, j)),
                  pl.BlockSpec((None, 1, tn), lambda l, j: (l, 0, j))],
        out_specs=pl.BlockSpec((None, r, tn), lambda l, j: (l, 0, j)),
        out_shape=SDS((depth, r, n), F32),
        compiler_params=_params(("arbitrary", "arbitrary")),
        name="mod",
    )(cond, w_mod, b_mod.reshape(depth, 1, n))


def _in_proj_kernel(x_ref, ln_ref, shift_ref, scale_ref, w_ref, ws_ref, o_ref, os_ref, h_ref):
    @pl.when(pl.program_id(1) == 0)
    def _():
        x = x_ref[...]
        y = x * lax.rsqrt(jnp.mean(x * x, axis=-1, keepdims=True) + EPS) * ln_ref[...]
        h = (y * (1.0 + scale_ref[...]) + shift_ref[...]).astype(BF16)
        h_ref[...] = h
        os_ref[...] = jnp.dot(h, ws_ref[...], preferred_element_type=F32)

    o_ref[...] = jnp.dot(h_ref[...], w_ref[...], preferred_element_type=F32).astype(o_ref.dtype)


def _in_proj_call(x, ln_w, mod3, w_main, w_small, layer, mod_row, tm):
    t, d = x.shape
    tn = w_main.shape[3]
    nm = w_main.shape[1] * tn
    ns = w_small.shape[2]
    return pl.pallas_call(
        _in_proj_kernel,
        grid=(t // tm, nm // tn),
        in_specs=[pl.BlockSpec((tm, d), lambda i, j: (i, 0)),
                  pl.BlockSpec((1, d), lambda i, j: (0, 0)),
                  pl.BlockSpec((None, 1, d), lambda i, j: (mod_row(i, tm) * N_MOD + 0, 0, 0)),
                  pl.BlockSpec((None, 1, d), lambda i, j: (mod_row(i, tm) * N_MOD + 1, 0, 0)),
                  pl.BlockSpec((None, None, d, tn), lambda i, j: (layer, j, 0, 0)),
                  pl.BlockSpec((None, d, ns), lambda i, j: (layer, 0, 0))],
        out_specs=[pl.BlockSpec((tm, tn), lambda i, j: (i, j)),
                   pl.BlockSpec((tm, ns), lambda i, j: (i, 0))],
        out_shape=[SDS((t, nm), BF16), SDS((t, ns), F32)],
        scratch_shapes=[pltpu.VMEM((tm, d), BF16)],
        compiler_params=_params(("arbitrary", "arbitrary")),
        name="in_proj",
    )(x, ln_w.reshape(1, d), mod3, mod3, w_main, w_small)


def _gla_iter(i, n, per, first, rank, q_ref, k_ref, v_ref, lr_ref, w2_ref, b2_ref, nw_ref, o_ref, st_ref, oacc_ref):
    dk = q_ref.shape[-1]
    jobs = [(s, (n - 1 - (i * per + u)) if s else (i * per + u)) for s in range(2) for u in range(per)]
    nj = len(jobs)
    rows = [pl.ds(pl.multiple_of(c * CHUNK, CHUNK), CHUNK) for s, c in jobs]
    incl = [_tri_masks(False)[0], _tri_masks(True)[0]]
    tri = [m.astype(BF16) for m in incl]
    w2 = [w2_ref[s].astype(BF16) for s in range(2)]
    q = [q_ref[rows[j], :].astype(F32) * dk ** -0.5 for j in range(nj)]
    k = [k_ref[rows[j], :].astype(F32) for j in range(nj)]
    v = [v_ref[rows[j], :] for j in range(nj)]
    lr = [lr_ref[rows[j], :][:, s * rank:(s + 1) * rank].astype(BF16) for j, (s, c) in enumerate(jobs)]
    pre = [jnp.dot(lr[j], w2[s], preferred_element_type=F32) + b2_ref[s] for j, (s, c) in enumerate(jobs)]
    g = [-_softplus(-x) / GLA_TAU for x in pre]
    cum = [_tri_dot(tri[s], g[j]) for j, (s, c) in enumerate(jobs)]
    cref = [cum[j][(CHUNK - 1 - CHUNK // 2 if s else CHUNK // 2):(CHUNK - CHUNK // 2 if s else CHUNK // 2 + 1), :]
            for j, (s, c) in enumerate(jobs)]
    clast = [cum[j][(0 if s else CHUNK - 1):(1 if s else CHUNK), :] for j, (s, c) in enumerate(jobs)]
    qa = [(q[j] * jnp.exp(cum[j] - cref[j])).astype(BF16) for j in range(nj)]
    ka = [(k[j] * jnp.exp(cref[j] - cum[j])).astype(BF16) for j in range(nj)]
    kd = [(k[j] * jnp.exp(clast[j] - cum[j])).astype(BF16) for j in range(nj)]
    qd = [(q[j] * jnp.exp(cum[j])).astype(BF16) for j in range(nj)]
    att = [lax.dot_general(qa[j], ka[j], _NT, preferred_element_type=F32) for j in range(nj)]
    kv = [lax.dot_general(v[j], kd[j], _TN, preferred_element_type=F32) for j in range(nj)]
    att = [jnp.where(incl[s], att[j], 0.0).astype(BF16) for j, (s, c) in enumerate(jobs)]
    o = [jnp.dot(att[j], v[j], preferred_element_type=F32) for j in range(nj)]
    st = [st_ref[0], st_ref[1]]
    for j, (s, c) in enumerate(jobs):
        o[j] = o[j] + lax.dot_general(qd[j], st[s].astype(BF16), _NT, preferred_element_type=F32)
        st[s] = st[s] * jnp.exp(clast[j]) + kv[j]
    for s in range(2):
        st_ref[s] = st[s]
    for j in range(nj):
        if first:
            oacc_ref[rows[j], :] = o[j]
        else:
            tot = oacc_ref[rows[j], :] + o[j]
            y = tot * lax.rsqrt(jnp.mean(tot * tot, axis=-1, keepdims=True) + EPS) * nw_ref[...]
            o_ref[rows[j], :] = y.astype(o_ref.dtype)


def _gla_kernel(*refs, rank, has_init, want_state):
    q_ref, k_ref, v_ref, lr_ref, w2_ref, b2_ref, nw_ref = refs[:7]
    refs = refs[7:]
    if has_init:
        s0_ref, refs = refs[0], refs[1:]
    o_ref, refs = refs[0], refs[1:]
    if want_state:
        sfin_ref, refs = refs[0], refs[1:]
    st_ref, oacc_ref = refs
    n = q_ref.shape[0] // CHUNK
    for s in range(2):
        if has_init:
            st_ref[s] = s0_ref[s].T
        else:
            st_ref[s] = jnp.zeros(st_ref.shape[1:], F32)
    common = (rank, q_ref, k_ref, v_ref, lr_ref, w2_ref, b2_ref, nw_ref, o_ref, st_ref, oacc_ref)
    per = 4 if n % 8 == 0 else (2 if n % 4 == 0 else 1)

    def half(first):
        def body(i, carry):
            _gla_iter(i, n, per, first, *common)
            return carry
        return body

    lax.fori_loop(0, n // (2 * per), half(True), 0)
    lax.fori_loop(n // (2 * per), n // per, half(False), 0)
    if want_state:
        for s in range(2):
            sfin_ref[s] = st_ref[s].T


def _gla_call(proj_main, proj_small, w2, b2, norm_w, state, layer, row0, batch, seq, heads, dk, dv, rank):
    has_init = state is not None
    want_state = not has_init
    qk_w = heads * dk
    assert seq % (2 * CHUNK) == 0 and (2 * qk_w) % dv == 0
    in_specs = [pl.BlockSpec((seq, dk), lambda b, h: (row0 + b, h)),
                pl.BlockSpec((seq, dk), lambda b, h: (row0 + b, heads + h)),
                pl.BlockSpec((seq, dv), lambda b, h: (row0 + b, 2 * qk_w // dv + h)),
                pl.BlockSpec((seq, proj_small.shape[1]), lambda b, h: (row0 + b, 0)),
                pl.BlockSpec((2, rank, dk), lambda b, h: (0, 0, h)),
                pl.BlockSpec((2, 1, dk), lambda b, h: (0, 0, h)),
                pl.BlockSpec((1, dv), lambda b, h: (0, 0))]
    args = [proj_main, proj_main, proj_main, proj_small, w2, b2.reshape(2, 1, qk_w), norm_w.reshape(1, dv)]
    if has_init:
        in_specs.append(pl.BlockSpec((None, None, 2, None, dk, dv), lambda b, h: (b, layer, 0, h, 0, 0)))
        args.append(state)
    out_specs = [pl.BlockSpec((seq, dv), lambda b, h: (b, h))]
    out_shape = [SDS((batch * seq, heads * dv), BF16)]
    if want_state:
        out_specs.append(pl.BlockSpec((None, 2, None, dk, dv), lambda b, h: (b, 0, h, 0, 0)))
        out_shape.append(SDS((batch, 2, heads, dk, dv), F32))
    outs = pl.pallas_call(
        functools.partial(_gla_kernel, rank=rank, has_init=has_init, want_state=want_state),
        grid=(batch, heads),
        in_specs=in_specs, out_specs=out_specs, out_shape=out_shape,
        scratch_shapes=[pltpu.VMEM((2, dv, dk), F32), pltpu.VMEM((seq, dv), F32)],
        compiler_params=_params(("arbitrary", "arbitrary")),
        name="gla",
    )(*args)
    return outs if want_state else (outs[0], None)


def _conv_kernel(x_ref, w_ref, o_ref, xs_ref, *, width, vertical, n_qk_blocks, n_q_blocks, qscale):
    seq = x_ref.shape[0]
    n_rows = seq // width
    cb = pl.program_id(1)
    w = w_ref[...]
    ridx = lax.broadcasted_iota(jnp.int32, (width, LANES), 0)
    tile_rows = lambda r: pl.ds(pl.multiple_of(r * width, width), width)

    for t in range(3):
        xs_ref[t, tile_rows(0), :] = jnp.zeros((width, LANES), F32)
        xs_ref[t, tile_rows(n_rows + 1), :] = jnp.zeros((width, LANES), F32)

    def shift_body(r, carry):
        tile = x_ref[tile_rows(r), :].astype(F32)
        xs_ref[1, tile_rows(r + 1), :] = tile
        for dc in (-1, 1):
            sh = pltpu.roll(tile, (-dc) % width, 0)
            edge = width - 1 if dc == 1 else 0
            xs_ref[dc + 1, tile_rows(r + 1), :] = jnp.where(ridx == edge, 0.0, sh)
        return carry

    unroll = max(1, min(n_rows, 8 * GRID_W // width))
    lax.fori_loop(0, n_rows, shift_body, 0, unroll=unroll)

    def body(r, carry):
        acc = jnp.zeros((width, LANES), F32)
        for dr in (-1, 0, 1):
            if not vertical and dr != 0:
                continue
            for dc in (-1, 0, 1):
                tap = (dr + 1) * 3 + (dc + 1)
                acc = acc + xs_ref[dc + 1, tile_rows(r + dr + 1), :] * w[tap:tap + 1, :]
        y = _silu(acc)
        yn = y * lax.rsqrt(jnp.sum(y * y, axis=-1, keepdims=True) + EPS)
        yn = yn * jnp.where(cb < n_q_blocks, qscale, 1.0)
        y = jnp.where(cb < n_qk_blocks, yn, y)
        o_ref[tile_rows(r), :] = y.astype(o_ref.dtype)
        return carry

    lax.fori_loop(0, n_rows, body, 0, unroll=unroll)


def _conv_call(proj_main, conv_w, col0, row0, batch, seq, width, vertical, heads, dk):
    ch = conv_w.shape[-1]
    nb = ch // LANES
    assert dk == LANES and col0 % LANES == 0 and seq % width == 0
    return pl.pallas_call(
        functools.partial(_conv_kernel, width=width, vertical=vertical, n_qk_blocks=2 * heads, n_q_blocks=heads,
                          qscale=dk ** -0.5),
        grid=(batch, nb),
        in_specs=[pl.BlockSpec((seq, LANES), lambda b, c: (row0 + b, col0 // LANES + c)),
                  pl.BlockSpec((9, LANES), lambda b, c: (0, c))],
        out_specs=pl.BlockSpec((seq, LANES), lambda b, c: (b, c)),
        out_shape=SDS((batch * seq, ch), BF16),
        scratch_shapes=[pltpu.VMEM((3, seq + 2 * width, LANES), F32)],
        compiler_params=_params(("arbitrary", "arbitrary")),
        name="gdn_conv",
    )(proj_main, conv_w.reshape(9, ch))


def _gdn_iter(cf, cb, first, hg, n_ab, q_ref, k_ref, v_ref, lr_ref, arow_ref, dtrow_ref, nw_ref, o_ref, st_ref,
              oacc_ref, sel_ref):
    d = LANES
    chains = [(s, hh) for s in range(2) for hh in range(hg)]
    rows = [pl.ds(pl.multiple_of(c * CHUNK, CHUNK), CHUNK) for c in (cf, cb)]
    masks = [_tri_masks(False), _tri_masks(True)]
    tri = [m[0].astype(BF16) for m in masks]
    lane = lax.broadcasted_iota(jnp.int32, (CHUNK, d), 1)
    strict_w = [jnp.logical_and(lane < CHUNK, r_c) for r_c in
                (lax.broadcasted_iota(jnp.int32, (CHUNK, d), 0) > lane, lax.broadcasted_iota(jnp.int32, (CHUNK, d), 0) < lane)]

    tiles = []
    for s in range(2):
        lr = lr_ref[rows[s], :]
        g_t = -jnp.exp(arow_ref[...]) * _softplus(lr + dtrow_ref[...])
        tiles.append(_split_bf16(jnp.where(lane < n_ab, g_t, _sigmoid(lr))))
    q = [q_ref[rows[s], hh * d:(hh + 1) * d] for s, hh in chains]
    k = [k_ref[rows[s], hh * d:(hh + 1) * d] for s, hh in chains]
    v = [v_ref[rows[s], hh * d:(hh + 1) * d].astype(F32) for s, hh in chains]
    kf = [x.astype(F32) for x in k]

    sel = [jnp.dot(tiles[s][0], sel_ref[g], preferred_element_type=F32)
           + jnp.dot(tiles[s][1], sel_ref[g], preferred_element_type=F32) for g, (s, hh) in enumerate(chains)]
    qkk = [lax.dot_general(jnp.concatenate([q[g], k[g]], axis=0), k[g], _NT, preferred_element_type=F32)
           for g in range(len(chains))]
    gb = [x[:, :d] for x in sel]
    beta = [x[:, d:] for x in sel]
    tr = [_tri_dot(tri[s], jnp.concatenate([gb[g], jnp.where(strict_w[s], gb[g], 0.0)], axis=1))
          for g, (s, hh) in enumerate(chains)]
    cum = [x[:, :d] for x in tr]
    gamma = [jnp.where(masks[s][0], jnp.exp(jnp.where(masks[s][0], tr[g][:, d:d + CHUNK], 0.0)), 0.0)
             for g, (s, hh) in enumerate(chains)]
    qk = [qkk[g][:CHUNK] * gamma[g] for g in range(len(chains))]
    m = [jnp.where(masks[s][1], beta[g][:, :CHUNK] * qkk[g][CHUNK:] * gamma[g], 0.0)
         for g, (s, hh) in enumerate(chains)]
    y = [-x for x in m]
    mb = [x.astype(BF16) for x in m]
    p = [jnp.dot(x, x, preferred_element_type=F32) for x in mb]
    steps = (CHUNK - 1).bit_length() - 1
    for it in range(steps):
        pb = [x.astype(BF16) for x in p]
        if it + 1 < steps:
            yp = [jnp.dot(jnp.concatenate([y[g].astype(BF16), pb[g]], axis=0), pb[g], preferred_element_type=F32)
                  for g in range(len(chains))]
            y = [y[g] + p[g] + yp[g][:CHUNK] for g in range(len(chains))]
            p = [x[CHUNK:] for x in yp]
        else:
            yp = [jnp.dot(y[g].astype(BF16), pb[g], preferred_element_type=F32) for g in range(len(chains))]
            y = [y[g] + p[g] + yp[g] for g in range(len(chains))]
    rhs = [jnp.concatenate([v[g] * beta[g], kf[g] * (beta[g] * jnp.exp(cum[g]))], axis=1) for g in range(len(chains))]
    sol = [rhs[g] + jnp.dot(y[g].astype(BF16), rhs[g].astype(BF16), preferred_element_type=F32)
           for g in range(len(chains))]
    clast = [cum[g][(0 if s else CHUNK - 1):(1 if s else CHUNK), :] for g, (s, hh) in enumerate(chains)]
    qd = [q[g].astype(F32) * jnp.exp(cum[g]) for g in range(len(chains))]
    kd = [(kf[g] * jnp.exp(clast[g] - cum[g])).astype(BF16) for g in range(len(chains))]
    st = [st_ref[s, hh] for s, hh in chains]
    wq = [jnp.dot(jnp.concatenate([sol[g][:, d:].astype(BF16), qd[g].astype(BF16)], axis=0), st[g].astype(BF16),
                  preferred_element_type=F32) for g in range(len(chains))]
    vb = [(sol[g][:, :d] - wq[g][:CHUNK]).astype(BF16) for g in range(len(chains))]
    o = [wq[g][CHUNK:] + jnp.dot(qk[g].astype(BF16), vb[g], preferred_element_type=F32) for g in range(len(chains))]
    st_new = [st[g] * jnp.exp(clast[g]) + lax.dot_general(kd[g], vb[g], _TN, preferred_element_type=F32)
              for g in range(len(chains))]
    for g, (s, hh) in enumerate(chains):
        st_ref[s, hh] = st_new[g]
        cols = slice(hh * d, (hh + 1) * d)
        if first:
            oacc_ref[rows[s], cols] = o[g]
        else:
            tot = oacc_ref[rows[s], cols] + o[g]
            yn = tot * lax.rsqrt(jnp.mean(tot * tot, axis=-1, keepdims=True) + EPS) * nw_ref[...]
            o_ref[rows[s], cols] = yn.astype(o_ref.dtype)


def _gdn_iter_paired(i, n, per, first, hg, n_ab, q_ref, k_ref, v_ref, lr_ref, arow_ref, dtrow_ref, nw_ref, o_ref,
                     st_ref, oacc_ref, sel_ref):
    d = LANES
    chains = [(s, u, hh) for s in range(2) for u in range(per) for hh in range(hg)]
    pairs = [(s, u, 2 * p) for s in range(2) for u in range(per) for p in range(hg // 2)]
    nc, npair = len(chains), len(pairs)
    rows = {(s, u): pl.ds(pl.multiple_of(((n - 1 - (i * per + u)) if s else (i * per + u)) * CHUNK, CHUNK), CHUNK)
            for s in range(2) for u in range(per)}
    tri = [_tri_masks(rev)[0].astype(BF16) for rev in (False, True)]
    lane = lax.broadcasted_iota(jnp.int32, (CHUNK, d), 1)
    row = lax.broadcasted_iota(jnp.int32, (CHUNK, d), 0)
    left = lane < CHUNK
    col = jnp.where(left, lane, lane - CHUNK)
    incl_p = [row >= col, row <= col]
    strict_p = [row > col, row < col]
    zero_b = jnp.zeros((CHUNK, d), BF16)

    def lanes(a, b):
        return jnp.concatenate([a, b], axis=1)

    def stack(a, b):
        return jnp.concatenate([a, b], axis=0)

    def bdiag(x):
        z = jnp.zeros_like(x)
        return stack(jnp.where(left, x, z), jnp.where(left, z, x))

    tiles = {}
    for su in rows:
        lr = lr_ref[rows[su], :]
        g_t = -jnp.exp(arow_ref[...]) * _softplus(lr + dtrow_ref[...])
        tiles[su] = _split_bf16(jnp.where(lane < n_ab, g_t, _sigmoid(lr)))
    q = [q_ref[rows[s, u], hh * d:(hh + 1) * d] for s, u, hh in chains]
    k = [k_ref[rows[s, u], hh * d:(hh + 1) * d] for s, u, hh in chains]
    v = [v_ref[rows[s, u], hh * d:(hh + 1) * d].astype(F32) for s, u, hh in chains]
    kf = [x.astype(F32) for x in k]

    sel = [jnp.dot(tiles[s, u][0], sel_ref[s * hg + hh], preferred_element_type=F32)
           + jnp.dot(tiles[s, u][1], sel_ref[s * hg + hh], preferred_element_type=F32) for s, u, hh in chains]
    qkk = [lax.dot_general(stack(lanes(q[2 * p], q[2 * p + 1]), lanes(k[2 * p], k[2 * p + 1])),
                           stack(lanes(k[2 * p], zero_b), lanes(zero_b, k[2 * p + 1])), _NT,
                           preferred_element_type=F32) for p in range(npair)]
    gb = [x[:, :d] for x in sel]
    beta = [x[:, d:] for x in sel]
    beta_p = [jnp.where(left, beta[2 * p], beta[2 * p + 1]) for p in range(npair)]
    tr = [_tri_dot(tri[s], jnp.concatenate(
        [gb[2 * p], gb[2 * p + 1], jnp.where(strict_p[s], jnp.where(left, gb[2 * p], gb[2 * p + 1]), 0.0)], axis=1))
        for p, (s, u, hh) in enumerate(pairs)]
    cum = [tr[g // 2][:, (g % 2) * d:(g % 2 + 1) * d] for g in range(nc)]
    gamma = [jnp.where(incl_p[s], jnp.exp(jnp.where(incl_p[s], tr[p][:, 2 * d:], 0.0)), 0.0)
             for p, (s, u, hh) in enumerate(pairs)]
    qk = [qkk[p][:CHUNK] * gamma[p] for p in range(npair)]
    m = [jnp.where(strict_p[s], beta_p[p] * qkk[p][CHUNK:] * gamma[p], 0.0) for p, (s, u, hh) in enumerate(pairs)]
    y = [-x for x in m]
    mb = [x.astype(BF16) for x in m]
    pw = [jnp.dot(x, bdiag(x), preferred_element_type=F32) for x in mb]
    steps = (CHUNK - 1).bit_length() - 1
    for it in range(steps):
        pb = [x.astype(BF16) for x in pw]
        if it + 1 < steps:
            yp = [jnp.dot(stack(y[p].astype(BF16), pb[p]), bdiag(pb[p]), preferred_element_type=F32)
                  for p in range(npair)]
            y = [y[p] + pw[p] + yp[p][:CHUNK] for p in range(npair)]
            pw = [x[CHUNK:] for x in yp]
        else:
            yp = [jnp.dot(y[p].astype(BF16), bdiag(pb[p]), preferred_element_type=F32) for p in range(npair)]
            y = [y[p] + pw[p] + yp[p] for p in range(npair)]
    rhs = [lanes(v[g] * beta[g], kf[g] * (beta[g] * jnp.exp(cum[g]))) for g in range(nc)]
    zero_r = jnp.zeros((CHUNK, 2 * d), BF16)
    ysol = [jnp.dot(y[p].astype(BF16), stack(lanes(rhs[2 * p].astype(BF16), zero_r), lanes(zero_r, rhs[2 * p + 1].astype(BF16))),
                    preferred_element_type=F32) for p in range(npair)]
    sol = [rhs[g] + ysol[g // 2][:, (g % 2) * 2 * d:(g % 2 + 1) * 2 * d] for g in range(nc)]
    clast = [cum[g][(0 if s else CHUNK - 1):(1 if s else CHUNK), :] for g, (s, u, hh) in enumerate(chains)]
    qd = [(q[g].astype(F32) * jnp.exp(cum[g])).astype(BF16) for g in range(nc)]
    kd = [(kf[g] * jnp.exp(clast[g] - cum[g])).astype(BF16) for g in range(nc)]
    wb = [sol[g][:, d:].astype(BF16) for g in range(nc)]
    dec = [jnp.exp(clast[g]) for g in range(nc)]
    st = {(s, hh): st_ref[s, hh] for s in range(2) for hh in range(hg)}
    for u in range(per):
        gs = [g for g, c in enumerate(chains) if c[1] == u]
        ps = [p for p, c in enumerate(pairs) if c[1] == u]
        wq = {g: jnp.dot(stack(wb[g], qd[g]), st[chains[g][0], chains[g][2]].astype(BF16),
                         preferred_element_type=F32) for g in gs}
        vb = {g: (sol[g][:, :d] - wq[g][:CHUNK]).astype(BF16) for g in gs}
        oq = {p: jnp.dot(qk[p].astype(BF16), stack(lanes(vb[2 * p], zero_b), lanes(zero_b, vb[2 * p + 1])),
                         preferred_element_type=F32) for p in ps}
        for g in gs:
            s, _, hh = chains[g]
            st[s, hh] = st[s, hh] * dec[g] + lax.dot_general(kd[g], vb[g], _TN, preferred_element_type=F32)
            o = wq[g][CHUNK:] + oq[g // 2][:, (g % 2) * d:(g % 2 + 1) * d]
            cols = slice(hh * d, (hh + 1) * d)
            if first:
                oacc_ref[rows[s, u], cols] = o
            else:
                tot = oacc_ref[rows[s, u], cols] + o
                yn = tot * lax.rsqrt(jnp.mean(tot * tot, axis=-1, keepdims=True) + EPS) * nw_ref[...]
                o_ref[rows[s, u], cols] = yn.astype(o_ref.dtype)
    for (s, hh), val in st.items():
        st_ref[s, hh] = val


def _gdn_kernel(*refs, hg, per_chunks, a_off, n_heads, has_init, want_state):
    q_ref, k_ref, v_ref, lr_ref, arow_ref, dtrow_ref, nw_ref = refs[:7]
    refs = refs[7:]
    if has_init:
        s0_ref, refs = refs[0], refs[1:]
    o_ref, refs = refs[0], refs[1:]
    if want_state:
        sfin_ref, refs = refs[0], refs[1:]
    st_ref, oacc_ref, sel_ref = refs
    n = q_ref.shape[0] // CHUNK
    if has_init:
        st_ref[...] = s0_ref[...]
    else:
        st_ref[...] = jnp.zeros(st_ref.shape, F32)
    grp = pl.program_id(1)
    r = lax.broadcasted_iota(jnp.int32, sel_ref.shape[1:], 0)
    c = lax.broadcasted_iota(jnp.int32, sel_ref.shape[1:], 1)
    for s in range(2):
        for hh in range(hg):
            head = grp * hg + hh
            col_a = a_off + s * n_heads + head
            col_b = a_off + (2 + s) * n_heads + head
            sel_ref[s * hg + hh] = jnp.where(r == jnp.where(c < LANES, col_a, col_b), 1.0, 0.0).astype(BF16)
    common = (hg, a_off + 2 * n_heads, q_ref, k_ref, v_ref, lr_ref, arow_ref, dtrow_ref, nw_ref, o_ref, st_ref,
              oacc_ref, sel_ref)

    paired = hg % 2 == 0
    per = per_chunks if (paired and n % (2 * per_chunks) == 0) else 1

    def half(first):
        def body(i, carry):
            if paired:
                _gdn_iter_paired(i, n, per, first, *common)
            else:
                _gdn_iter(i, n - 1 - i, first, *common)
            return carry
        return body

    lax.fori_loop(0, n // (2 * per), half(True), 0)
    lax.fori_loop(n // (2 * per), n // per, half(False), 0)
    if want_state:
        sfin_ref[...] = st_ref[...]


def _gdn_call(conv, proj_small, a_log, dt_bias, norm_w, state, layer, row0, batch, seq, heads, hg, a_off,
              per_chunks=1):
    has_init = state is not None
    want_state = not has_init
    d = LANES
    ng = heads // hg
    small_w = proj_small.shape[1]
    assert heads % hg == 0 and seq % (2 * CHUNK) == 0 and small_w == LANES and a_off + 4 * heads <= small_w
    pad = lambda x: jnp.pad(x.reshape(1, 2 * heads), ((0, 0), (a_off, small_w - a_off - 2 * heads)))
    wide = seq * hg * d
    fits = (3 * 2 * 2 + 2 * 2 + 4) * wide <= VMEM_LIMIT * 3 // 4
    one = {} if fits else dict(pipeline_mode=pl.Buffered(1))
    in_specs = [pl.BlockSpec((seq, hg * d), lambda b, g: (b, g), **one),
                pl.BlockSpec((seq, hg * d), lambda b, g: (b, ng + g), **one),
                pl.BlockSpec((seq, hg * d), lambda b, g: (b, 2 * ng + g), **one),
                pl.BlockSpec((seq, small_w), lambda b, g: (row0 + b, 0), **one),
                pl.BlockSpec((1, small_w), lambda b, g: (0, 0)),
                pl.BlockSpec((1, small_w), lambda b, g: (0, 0)),
                pl.BlockSpec((1, d), lambda b, g: (0, 0))]
    args = [conv, conv, conv, proj_small, pad(a_log), pad(dt_bias), norm_w.reshape(1, d)]
    if has_init:
        in_specs.append(pl.BlockSpec((None, None, 2, hg, d, d), lambda b, g: (b, layer, 0, g, 0, 0), **one))
        args.append(state)
    out_specs = [pl.BlockSpec((seq, hg * d), lambda b, g: (b, g), **one)]
    out_shape = [SDS((batch * seq, heads * d), BF16)]
    if want_state:
        out_specs.append(pl.BlockSpec((None, 2, hg, d, d), lambda b, g: (b, 0, g, 0, 0)))
        out_shape.append(SDS((batch, 2, heads, d, d), F32))
    outs = pl.pallas_call(
        functools.partial(_gdn_kernel, hg=hg, per_chunks=per_chunks, a_off=a_off, n_heads=heads, has_init=has_init,
                          want_state=want_state),
        grid=(batch, ng),
        in_specs=in_specs, out_specs=out_specs, out_shape=out_shape,
        scratch_shapes=[pltpu.VMEM((2, hg, d, d), F32), pltpu.VMEM((seq, hg * d), F32),
                        pltpu.VMEM((2 * hg, LANES, 2 * LANES), BF16)],
        compiler_params=_params(("arbitrary", "arbitrary"), VMEM_LIMIT if fits else VMEM_LIMIT_MAX),
        name="gdn",
    )(*args)
    return outs if want_state else (outs[0], None)


def _out_proj_kernel(gl_ref, gc_ref, dl_ref, dc_ref, gr_ref, dz_ref, x_ref, gate_ref, w_ref, o_ref, lhs_ref,
                     *, n_lat_blocks):
    i = pl.program_id(0)
    half = gl_ref.shape[1]

    def fill(g_ref, d_ref):
        lhs_ref[:, :half] = (g_ref[...].astype(F32) * _silu(gr_ref[...].astype(F32))).astype(BF16)
        lhs_ref[:, half:] = (d_ref[...].astype(F32) * _silu(dz_ref[...].astype(F32))).astype(BF16)

    @pl.when(jnp.logical_and(pl.program_id(1) == 0, i < n_lat_blocks))
    def _():
        fill(gl_ref, dl_ref)

    @pl.when(jnp.logical_and(pl.program_id(1) == 0, i >= n_lat_blocks))
    def _():
        fill(gc_ref, dc_ref)

    mix = jnp.dot(lhs_ref[...], w_ref[...], preferred_element_type=F32)
    o_ref[...] = x_ref[...] + gate_ref[...] * mix


def _out_proj_call(og_lat, og_ctx, od_lat, od_ctx, proj_main, gr_col, dz_col, x, mod3, w_out, layer, mod_row, tm):
    t, d = x.shape
    tn = w_out.shape[3]
    half = og_lat.shape[1]
    nl = og_lat.shape[0] // tm
    nc = og_ctx.shape[0] // tm
    assert gr_col % half == 0 and dz_col % half == 0 and nl * tm == og_lat.shape[0] and nc * tm == og_ctx.shape[0]
    lat_map = lambda i, j: (jnp.minimum(i, nl - 1), 0)
    ctx_map = lambda i, j: (jnp.clip(i - nl, 0, nc - 1), 0)
    return pl.pallas_call(
        functools.partial(_out_proj_kernel, n_lat_blocks=nl),
        grid=(t // tm, d // tn),
        in_specs=[pl.BlockSpec((tm, half), lat_map), pl.BlockSpec((tm, half), ctx_map),
                  pl.BlockSpec((tm, half), lat_map), pl.BlockSpec((tm, half), ctx_map),
                  pl.BlockSpec((tm, half), lambda i, j: (i, gr_col // half)),
                  pl.BlockSpec((tm, half), lambda i, j: (i, dz_col // half)),
                  pl.BlockSpec((tm, tn), lambda i, j: (i, j)),
                  pl.BlockSpec((None, 1, tn), lambda i, j: (mod_row(i, tm) * N_MOD + 2, 0, j)),
                  pl.BlockSpec((None, None, 2 * half, tn), lambda i, j: (layer, j, 0, 0))],
        out_specs=pl.BlockSpec((tm, tn), lambda i, j: (i, j)),
        out_shape=SDS((t, d), F32),
        scratch_shapes=[pltpu.VMEM((tm, 2 * half), BF16)],
        compiler_params=_params(("arbitrary", "arbitrary")),
        name="out_proj",
    )(og_lat, og_ctx, od_lat, od_ctx, proj_main, proj_main, x, mod3, w_out)


def _pick4(sel, vals):
    return jnp.where(sel == 0, vals[0], jnp.where(sel == 1, vals[1], jnp.where(sel == 2, vals[2], vals[3])))


def _router_kernel(x_ref, ln_ref, shift_ref, scale_ref, rw_ref, rb_ref, hp_ref, idx_ref, gate_ref, *, n_experts):
    x = x_ref[...]
    y = x * lax.rsqrt(jnp.mean(x * x, axis=-1, keepdims=True) + EPS) * ln_ref[...]
    h = y * (1.0 + scale_ref[...]) + shift_ref[...]
    hb = h.astype(BF16)
    half = h.shape[1] // 2
    bits = pltpu.bitcast(hb.astype(F32), jnp.uint32)
    hp_ref[...] = lax.shift_right_logical(bits[:, :half], jnp.uint32(16)) | bits[:, half:]
    h_hi, h_lo = hb, (h - hb.astype(F32)).astype(BF16)
    w_hi, w_lo = _split_bf16(rw_ref[...])
    logits = (lax.dot_general(w_hi, h_hi, _NT, preferred_element_type=F32)
              + lax.dot_general(w_lo, h_hi, _NT, preferred_element_type=F32)
              + lax.dot_general(w_hi, h_lo, _NT, preferred_element_type=F32))
    mx = jnp.max(logits, axis=0, keepdims=True)
    ex = jnp.exp(logits - mx)
    probs = ex / jnp.sum(ex, axis=0, keepdims=True)
    sel = probs + rb_ref[...]
    per = n_experts // N_GROUPS
    assert per == 4 and TOP_K == 2
    rows_s = [sel[e:e + 1, :] for e in range(n_experts)]
    rows_p = [probs[e:e + 1, :] for e in range(n_experts)]
    best = None
    best_score = None
    for g in range(N_GROUPS):
        r = rows_s[g * per:(g + 1) * per]
        score = None
        for a in range(per):
            for b in range(a + 1, per):
                pair = r[a] + r[b]
                score = pair if score is None else jnp.maximum(score, pair)
        if g == 0:
            best, best_score = jnp.zeros_like(score, dtype=jnp.int32), score
        else:
            better = score > best_score
            best = jnp.where(better, g, best)
            best_score = jnp.where(better, score, best_score)
    sg = [_pick4(best, [rows_s[g * per + a] for g in range(N_GROUPS)]) for a in range(per)]
    pg = [_pick4(best, [rows_p[g * per + a] for g in range(N_GROUPS)]) for a in range(per)]
    i1 = jnp.zeros_like(best)
    m1 = sg[0]
    for a in range(1, per):
        better = sg[a] > m1
        i1 = jnp.where(better, a, i1)
        m1 = jnp.where(better, sg[a], m1)
    i2 = None
    m2 = None
    for a in range(per):
        cand = jnp.where(i1 == a, -jnp.inf, sg[a])
        if i2 is None:
            i2, m2 = jnp.zeros_like(best), cand
        else:
            better = cand > m2
            i2 = jnp.where(better, a, i2)
            m2 = jnp.where(better, cand, m2)
    p1 = _pick4(i1, pg)
    p2 = _pick4(i2, pg)
    tot = p1 + p2
    idx_ref[0:1, :] = best * per + i1
    idx_ref[1:2, :] = best * per + i2
    gate_ref[0:1, :] = p1 / tot
    gate_ref[1:2, :] = p2 / tot


def _router_call(x, ln_w, mod3, router_w, router_bias, mod_row, tm):
    t, d = x.shape
    e = router_w.shape[1]
    return pl.pallas_call(
        functools.partial(_router_kernel, n_experts=e),
        grid=(t // tm,),
        in_specs=[pl.BlockSpec((tm, d), lambda i: (i, 0)),
                  pl.BlockSpec((1, d), lambda i: (0, 0)),
                  pl.BlockSpec((None, 1, d), lambda i: (mod_row(i, tm) * N_MOD + 3, 0, 0)),
                  pl.BlockSpec((None, 1, d), lambda i: (mod_row(i, tm) * N_MOD + 4, 0, 0)),
                  pl.BlockSpec((e, d), lambda i: (0, 0)),
                  pl.BlockSpec((e, 1), lambda i: (0, 0))],
        out_specs=[pl.BlockSpec((tm, d // 2), lambda i: (i, 0)),
                   pl.BlockSpec((TOP_K, tm), lambda i: (0, i)),
                   pl.BlockSpec((TOP_K, tm), lambda i: (0, i))],
        out_shape=[SDS((t, d // 2), jnp.uint32), SDS((TOP_K, t), jnp.int32), SDS((TOP_K, t), F32)],
        compiler_params=_params(("arbitrary",)),
        name="router",
    )(x, ln_w.reshape(1, d), mod3, mod3, router_w.T, router_bias.reshape(e, 1))


def _moe_kernel(be_ref, nused_ref, tok0_ref, tokn_ref, dst_ref, hp_ref, w1_ref, w3_ref, w2_ref, g_ref, y_ref,
                xbuf_ref, xb_ref, act_ref, obuf_ref, gsem, ssem):
    i = pl.program_id(0)
    s = pl.program_id(1)
    n_steps = pl.num_programs(1)
    n_up, rows, tf = act_ref.shape
    tn = w2_ref.shape[1]
    n_down = obuf_ref.shape[1] // tn
    nused = nused_ref[0]
    used = i < nused
    slot = lax.rem(i, 2)
    g_per = rows // (n_up + n_down)
    s_per = rows // n_up

    def gather(tab_ref, r, sl):
        return pltpu.make_async_copy(hp_ref.at[pl.ds(tab_ref[0, r], 1), :], xbuf_ref.at[sl, pl.ds(r, 1), :], gsem.at[sl])

    def scatter(r):
        return pltpu.make_async_copy(obuf_ref.at[pl.ds(r, 1), :], y_ref.at[pl.ds(dst_ref[0, r], 1), :], ssem)

    def gather_wait(sl):
        pltpu.make_async_copy(hp_ref.at[pl.ds(0, rows), :], xbuf_ref.at[sl], gsem.at[sl]).wait()

    def scatter_wait():
        pltpu.make_async_copy(obuf_ref, y_ref.at[pl.ds(0, rows), :], ssem).wait()

    def gather_next():
        for r in range(g_per):
            gather(tokn_ref, s * g_per + r, 1 - slot).start()

    @pl.when(jnp.logical_and(i == 0, s == 0))
    def _():
        obuf_ref[...] = jnp.zeros(obuf_ref.shape, F32)

        def start(r, carry):
            gather(tok0_ref, r, 0).start()
            return carry

        lax.fori_loop(0, rows, start, 0, unroll=8)

    @pl.when(jnp.logical_and(used, s == 0))
    def _():
        gather_wait(slot)
        words = xbuf_ref[slot]
        half = words.shape[1]
        lo = pltpu.bitcast(lax.shift_left(words, jnp.uint32(16)), F32)
        hi = pltpu.bitcast(words & jnp.uint32(0xFFFF0000), F32)
        xb_ref[:, :half] = lo.astype(BF16)
        xb_ref[:, half:] = hi.astype(BF16)

    @pl.when(jnp.logical_and(used, s < n_up))
    def _():
        xb = xb_ref[...]
        a = jnp.dot(xb, w1_ref[...], preferred_element_type=F32)
        b = jnp.dot(xb, w3_ref[...], preferred_element_type=F32)
        act_ref[s] = (_silu(a) * b).astype(BF16)
        gather_next()
        for r in range(s_per):
            scatter(s * s_per + r).start()

    @pl.when(jnp.logical_and(used, s == n_up))
    def _():
        scatter_wait()

    @pl.when(jnp.logical_and(used, s >= n_up))
    def _():
        acc = jnp.dot(act_ref[0], w2_ref[0:tf, :], preferred_element_type=F32)
        for f in range(1, n_up):
            acc = acc + jnp.dot(act_ref[f], w2_ref[f * tf:(f + 1) * tf, :], preferred_element_type=F32)
        val = acc * g_ref[...]
        gather_next()
        obuf_ref[:, pl.ds(pl.multiple_of((s - n_up) * tn, tn), tn)] = val

    @pl.when(jnp.logical_and(i == nused, s == 0))
    def _():
        gather_wait(slot)

        def start(r, carry):
            scatter(r).start()
            return carry

        lax.fori_loop(0, rows, start, 0, unroll=8)
        scatter_wait()


def _moe_call(block_e, nused, slot_tok, slot_dst, slot_gate, hp, w1, w3, w2, layer, n_out_rows, rows):
    _, n_e, n_up, d, tf = w1.shape
    _, _, n_down, dff, tn = w2.shape
    nb = slot_tok.shape[0] // rows
    assert rows % (n_up + n_down) == 0 and rows % n_up == 0 and slot_dst.shape[0] == (nb + 1) * rows

    def blk(i, nu):
        return jnp.minimum(i, nu[0] - 1)

    def up_map(i, s, be, nu):
        return (layer, be[blk(i, nu)], jnp.where(i < nu[0], jnp.minimum(s, n_up - 1), n_up - 1), 0, 0)

    def down_map(i, s, be, nu):
        return (layer, be[blk(i, nu)], jnp.where(i < nu[0], jnp.maximum(s - n_up, 0), n_down - 1), 0, 0)

    smem_tab = lambda f: pl.BlockSpec((None, 1, rows), f, memory_space=pltpu.SMEM)
    return pl.pallas_call(
        _moe_kernel,
        grid_spec=pltpu.PrefetchScalarGridSpec(
            num_scalar_prefetch=2,
            grid=(nb, n_up + n_down),
            in_specs=[smem_tab(lambda i, s, be, nu: (0, 0, 0)),
                      smem_tab(lambda i, s, be, nu: (jnp.minimum(i + 1, nb - 1), 0, 0)),
                      smem_tab(lambda i, s, be, nu: (i, 0, 0)),
                      pl.BlockSpec(memory_space=pl.ANY),
                      pl.BlockSpec((None, None, None, d, tf), up_map),
                      pl.BlockSpec((None, None, None, d, tf), up_map),
                      pl.BlockSpec((None, None, None, dff, tn), down_map),
                      pl.BlockSpec((rows, 1), lambda i, s, be, nu: (blk(i, nu), 0))],
            out_specs=pl.BlockSpec(memory_space=pl.ANY),
            scratch_shapes=[pltpu.VMEM((2, rows, d // 2), jnp.uint32), pltpu.VMEM((rows, d), BF16),
                            pltpu.VMEM((n_up, rows, tf), BF16), pltpu.VMEM((rows, d), F32),
                            pltpu.SemaphoreType.DMA((2,)), pltpu.SemaphoreType.DMA(())]),
        out_shape=SDS((n_out_rows, d), F32),
        compiler_params=_params(("arbitrary", "arbitrary")),
        name="moe_ffn",
    )(block_e, nused, slot_tok.reshape(nb, 1, rows), slot_tok.reshape(nb, 1, rows),
      slot_dst.reshape(nb + 1, 1, rows), hp, w1, w3, w2, slot_gate.reshape(nb * rows, 1))


def _combine_kernel(x_ref, gate_ref, fw_ref, y0_ref, y1_ref, *o_refs, n_lat_blocks):
    v = x_ref[...] + gate_ref[...] * (y0_ref[...] + y1_ref[...])
    if len(o_refs) == 1:
        o_refs[0][...] = v
        return
    v = v * lax.rsqrt(jnp.mean(v * v, axis=-1, keepdims=True) + EPS) * fw_ref[...]

    @pl.when(pl.program_id(0) < n_lat_blocks)
    def _():
        o_refs[0][...] = v

    @pl.when(pl.program_id(0) >= n_lat_blocks)
    def _():
        o_refs[1][...] = v


def _combine_call(x, mod3, final_w, ys, mod_row, tm, t_lat, final):
    t, d = x.shape
    nl = t_lat // tm
    nc = (t - t_lat) // tm
    if final:
        out_specs = [pl.BlockSpec((tm, d), lambda i: (jnp.minimum(i, nl - 1), 0)),
                     pl.BlockSpec((tm, d), lambda i: (jnp.maximum(i - nl, 0), 0))]
        out_shape = [SDS((t_lat, d), F32), SDS((t - t_lat, d), F32)]
    else:
        out_specs = [pl.BlockSpec((tm, d), lambda i: (i, 0))]
        out_shape = [SDS((t, d), F32)]
    assert nl * tm == t_lat and (nl + nc) * tm == t
    outs = pl.pallas_call(
        functools.partial(_combine_kernel, n_lat_blocks=nl),
        grid=(t // tm,),
        in_specs=[pl.BlockSpec((tm, d), lambda i: (i, 0)),
                  pl.BlockSpec((None, 1, d), lambda i: (mod_row(i, tm) * N_MOD + 5, 0, 0)),
                  pl.BlockSpec((1, d), lambda i: (0, 0)),
                  pl.BlockSpec((tm, d), lambda i: (i, 0)),
                  pl.BlockSpec((tm, d), lambda i: (t // tm + i, 0))],
        out_specs=out_specs, out_shape=out_shape,
        compiler_params=_params(("arbitrary",)),
        name="moe_combine",
    )(x, mod3, final_w.reshape(1, d), ys, ys)
    return outs if final else outs[0]


def _dispatch_tables(idx, gate, n_experts, rows):
    t = idx.shape[1]
    n_assign = t * TOP_K
    flat_e = idx.T.reshape(-1)
    flat_g = gate.T.reshape(-1)
    order = jnp.argsort(flat_e, stable=True).astype(jnp.int32)
    counts = jnp.sum((flat_e[:, None] == jnp.arange(n_experts, dtype=jnp.int32)[None, :]).astype(jnp.int32), axis=0)
    padded = (counts + rows - 1) // rows * rows
    pad_end = jnp.cumsum(padded)
    pad_start = pad_end - padded
    start = jnp.cumsum(counts) - counts
    n_blocks = -(-n_assign // rows) + n_experts
    n_slots = n_blocks * rows
    block_e = jnp.minimum(jnp.searchsorted(pad_end, jnp.arange(n_blocks, dtype=jnp.int32) * rows, side='right'),
                          n_experts - 1).astype(jnp.int32)
    nused = (pad_end[-1] // rows).astype(jnp.int32).reshape(1)
    slot = jnp.arange(n_slots, dtype=jnp.int32)
    slot_e = jnp.repeat(block_e, rows)
    within = slot - pad_start[slot_e]
    valid = jnp.logical_and(within < counts[slot_e], slot < pad_end[-1])
    src = order[jnp.clip(start[slot_e] + within, 0, n_assign - 1)]
    slot_tok = jnp.where(valid, src // TOP_K, 0).astype(jnp.int32)
    slot_gate = jnp.where(valid, flat_g[src], 0.0)
    dump = n_assign + slot % rows
    slot_dst = jnp.where(valid, (src % TOP_K) * t + src // TOP_K, dump).astype(jnp.int32)
    slot_dst = jnp.concatenate([n_assign + jnp.arange(rows, dtype=jnp.int32), slot_dst])
    return block_e, nused, slot_tok, slot_dst, slot_gate


def _tile(n, target):
    if n <= target:
        return n
    best = None
    for cand in range(LANES, target + 1, LANES):
        if n % cand == 0:
            best = cand
    assert best is not None, (n, target)
    return best


def kernel(x_prompt, x_sample, state_gla, state_gdn, c, c_ctx, ln1_w, w_mod, b_mod, w_in, gla_w2, gla_b2, gla_norm_w, gdn_conv_w, gdn_a_log, gdn_dt_bias, gdn_norm_w, w_out, ln2_w, router_w, router_bias, w1, w3, w2, final_norm_w):
    bc, lc, d = x_prompt.shape
    bl, ll, _ = x_sample.shape
    depth = w_in.shape[0]
    gh, gdk, gdv = state_gla.shape[3:]
    dh, ddk, ddv = state_gdn.shape[3:]
    rank = gla_w2.shape[2]
    n_experts = router_w.shape[1]
    qk_w, d_gla, d_gdn = gh * gdk, gh * gdv, dh * ddv
    t_lat, t_ctx = bl * ll, bc * lc
    assert ddk == LANES and ddv == LANES and t_lat % t_ctx == 0 and ll % GRID_W == 0

    x = jnp.concatenate([x_sample.reshape(t_lat, d), x_prompt.reshape(t_ctx, d)], axis=0)
    n_cond = -(-(bl + 1) // 8) * 8
    cond = jnp.concatenate([c, c_ctx[None, :], jnp.zeros((n_cond - bl - 1, d), F32)], axis=0)
    mod = _mod_call(cond, w_mod, b_mod)

    def mod_row(i, tm):
        return jnp.minimum(i * tm // ll, bl)

    o_glr = 2 * qk_w + 2 * d_gla
    o_dqkv = o_glr + 2 * rank
    o_da = o_dqkv + 4 * d_gdn
    n_small = 2 * rank + 4 * dh
    small_w = -(-n_small // LANES) * LANES
    gr_col = 2 * qk_w + d_gla
    dqkv_col = gr_col + d_gla
    dz_col = dqkv_col + 3 * d_gdn

    tm = math.gcd(math.gcd(ll, t_ctx), 512)
    w_main = jnp.concatenate([w_in[:, :, :o_glr], w_in[:, :, o_dqkv:o_da]], axis=2).astype(BF16)
    w_small = jnp.concatenate([w_in[:, :, o_glr:o_dqkv], w_in[:, :, o_da:],
                               jnp.zeros((depth, d, small_w - n_small), F32)], axis=2).astype(BF16)
    def tile_major(w, t):
        lead, (r, c) = w.shape[:-2], w.shape[-2:]
        k = len(lead)
        return w.astype(BF16).reshape(*lead, r, c // t, t).transpose(*range(k), k + 1, k, k + 2)

    tn_in, tn_out = _tile(w_main.shape[2], 1024), _tile(d, 1024)
    w_main = tile_major(w_main, tn_in)
    w_out_b = tile_major(w_out, tn_out)
    w1_b, w3_b = (tile_major(w, _tile(w1.shape[-1], 512)) for w in (w1, w3))
    w2_b = tile_major(w2, _tile(d, 1024))
    gla_states, gdn_states = [], []
    for l in range(depth):
        mod3 = mod[l].reshape(n_cond * N_MOD, 1, d)
        proj_main, proj_small = _in_proj_call(x, ln1_w[l], mod3, w_main, w_small, l, mod_row, tm)

        og_lat, _ = _gla_call(proj_main, proj_small, gla_w2[l], gla_b2[l], gla_norm_w[l], state_gla, l,
                              0, bl, ll, gh, gdk, gdv, rank)
        og_ctx, sg = _gla_call(proj_main, proj_small, gla_w2[l], gla_b2[l], gla_norm_w[l], None, l,
                               t_lat // lc, bc, lc, gh, gdk, gdv, rank)
        gla_states.append(sg)

        conv_lat = _conv_call(proj_main, gdn_conv_w[l], dqkv_col, 0, bl, ll, GRID_W, True, dh, ddk)
        conv_ctx = _conv_call(proj_main, gdn_conv_w[l], dqkv_col, t_lat // t_ctx, 1, t_ctx, lc, False, dh, ddk)
        hg = math.gcd(dh, 8)
        od_lat, _ = _gdn_call(conv_lat, proj_small, gdn_a_log[l], gdn_dt_bias[l], gdn_norm_w[l], state_gdn, l,
                              0, bl, ll, dh, hg, 2 * rank, per_chunks=2)
        od_ctx, sd = _gdn_call(conv_ctx, proj_small, gdn_a_log[l], gdn_dt_bias[l], gdn_norm_w[l], None, l,
                               t_lat // lc, bc, lc, dh, hg, 2 * rank, per_chunks=2)
        gdn_states.append(sd)

        x1 = _out_proj_call(og_lat, og_ctx, od_lat, od_ctx, proj_main, gr_col, dz_col, x, mod3,
                            w_out_b, l, mod_row, tm)

        hp, idx, gate = _router_call(x1, ln2_w[l], mod3, router_w, router_bias, mod_row, tm)
        block_e, nused, slot_tok, slot_dst, slot_gate = _dispatch_tables(idx, gate, n_experts, MOE_ROWS)
        ys = _moe_call(block_e, nused, slot_tok, slot_dst, slot_gate, hp, w1_b, w3_b, w2_b, l,
                       TOP_K * (t_lat + t_ctx) + MOE_ROWS, MOE_ROWS)
        x = _combine_call(x1, mod3, final_norm_w, ys, mod_row, _tile(tm, 256), t_lat, l == depth - 1)

    y_sample = x[0].reshape(bl, ll, d)
    y_prompt = x[1].reshape(bc, lc, d)
    return (y_prompt, y_sample, jnp.stack(gla_states, axis=1), jnp.stack(gdn_states, axis=1))
```
